```python
import jax, jax.numpy as jnp
from jax import lax
import numpy as np

D_MODEL = 1024
BATCH = 16
SEQ = 4096
DEPTH = 4
DEC_BATCH = 16
DEC_SEQ = 16
PAST_LEN = 1024

CHUNK = 64
N_MIXERS = 3
N_GLA_LAYERS = (DEPTH + 2) // 3
N_RET_LAYERS = (DEPTH + 1) // 3
N_SWA_LAYERS = DEPTH // 3

GLA_HEADS = 4
GLA_KEY_DIM = D_MODEL // 2
GLA_DK = GLA_KEY_DIM // GLA_HEADS
GLA_DV = D_MODEL // GLA_HEADS
GLA_GATE_RANK = 16
GLA_TAU = 16.0

RET_HEADS = D_MODEL // 256
RET_DK = D_MODEL // RET_HEADS
RET_VALUE_DIM = 2 * D_MODEL
RET_DV = RET_VALUE_DIM // RET_HEADS

SWA_Q_HEADS = 16
SWA_KV_HEADS = 4
SWA_GROUP = SWA_Q_HEADS // SWA_KV_HEADS
SWA_HEAD_DIM = D_MODEL // SWA_Q_HEADS
WINDOW = 128
SWA_CHUNKS = WINDOW // CHUNK
SWA_CACHE_LEN = min(WINDOW, PAST_LEN)

MLP_HIDDEN = 4 * D_MODEL
ADA_CHUNKS = 6
NORM_EPS = 1e-6
GN_EPS = 1e-5
NEG_INF = -1e30

kernel_name = 'chunk_streaming_hybrid_gla_retnet_swa'


def rms_norm(x, g):
    xf = x.astype(jnp.float32)
    y = xf * lax.rsqrt(jnp.mean(jnp.square(xf), axis=-1, keepdims=True) + NORM_EPS)
    return (y * g.astype(jnp.float32)).astype(x.dtype)


def head_group_norm(x):
    xf = x.astype(jnp.float32)
    mu = jnp.mean(xf, axis=-1, keepdims=True)
    var = jnp.mean(jnp.square(xf - mu), axis=-1, keepdims=True)
    return ((xf - mu) * lax.rsqrt(var + GN_EPS)).astype(x.dtype)


def ada_modulation(c, w, b):
    mod = jax.nn.silu(c) @ w + b
    return jnp.split(mod[:, None, :], ADA_CHUNKS, axis=-1)


def modulate(x, g, shift, scale):
    return rms_norm(x, g) * (1.0 + scale) + shift


def sq_relu_mlp(h, w1, w2):
    return jnp.square(jax.nn.relu(h @ w1)) @ w2


def to_chunks(x, L):
    B, T, H, d = x.shape
    return x.reshape(B, T // L, L, H, d).transpose(1, 0, 3, 2, 4)


def from_chunks(x):
    NC, B, H, L, d = x.shape
    return x.transpose(1, 0, 3, 2, 4).reshape(B, NC * L, H, d)


def gla_recurrence(q, k, v, logg, S0):
    T = q.shape[1]
    L = min(CHUNK, T)
    xs = tuple(to_chunks(a.astype(jnp.float32), L) for a in (q, k, v, logg))
    causal = jnp.tril(jnp.ones((L, L), dtype=bool))

    def step(S, inp):
        qc, kc, vc, gc = inp
        b = jnp.cumsum(gc, axis=-2)
        bL = b[..., -1:, :]
        qd = qc * jnp.exp(b)
        kd = kc * jnp.exp(-b)
        att = jnp.where(causal, jnp.einsum('bhld,bhsd->bhls', qd, kd), 0.0)
        o = jnp.einsum('bhls,bhsv->bhlv', att, vc) + jnp.einsum('bhld,bhdv->bhlv', qd, S)
        S = S * jnp.exp(jnp.swapaxes(bL, -1, -2)) + jnp.einsum('bhsd,bhsv->bhdv', kc * jnp.exp(bL - b), vc)
        return S, o

    S, o = lax.scan(step, S0.astype(jnp.float32), xs)
    return from_chunks(o).astype(q.dtype), S


def gla_mixer(h, S0, wq, wk, wv, wg1, wg2, bg, wr, norm_g, wo):
    B, T, _ = h.shape
    q = (h @ wq).reshape(B, T, GLA_HEADS, GLA_DK) * GLA_DK ** -0.5
    k = (h @ wk).reshape(B, T, GLA_HEADS, GLA_DK)
    v = (h @ wv).reshape(B, T, GLA_HEADS, GLA_DV)
    logg = jax.nn.log_sigmoid(((h @ wg1) @ wg2 + bg).astype(jnp.float32)).reshape(B, T, GLA_HEADS, GLA_DK) / GLA_TAU
    o, S = gla_recurrence(q, k, v, logg, S0)
    o = rms_norm(o, norm_g).reshape(B, T, D_MODEL) * jax.nn.silu(h @ wr)
    return o @ wo, S


def retention_log_decay():
    return jnp.log(1.0 - 2.0 ** (-5.0 - jnp.arange(RET_HEADS, dtype=jnp.float32)))


def retention_recurrence(q, k, v, S0):
    T = q.shape[1]
    L = min(CHUNK, T)
    lg = retention_log_decay()
    pos = jnp.arange(L, dtype=jnp.float32)
    rel = pos[:, None] - pos[None, :]
    decay_intra = jnp.where(rel >= 0, jnp.exp(lg[:, None, None] * jnp.maximum(rel, 0.0)), 0.0)
    decay_query = jnp.exp(lg[:, None] * (pos + 1.0))[None, :, :, None]
    decay_key = jnp.exp(lg[:, None] * (L - 1.0 - pos))[:, :, None]
    decay_chunk = jnp.exp(lg * L)[:, None, None]
    xs = tuple(to_chunks(a.astype(jnp.float32), L) for a in (q, k, v))

    def step(S, inp):
        qc, kc, vc = inp
        att = jnp.einsum('bhld,bhsd->bhls', qc, kc) * decay_intra
        o = jnp.einsum('bhls,bhsv->bhlv', att, vc) + jnp.einsum('bhld,bhdv->bhlv', qc, S) * decay_query
        S = S * decay_chunk + jnp.einsum('bhsd,bhsv->bhdv', kc * decay_key, vc)
        return S, o

    S, o = lax.scan(step, S0.astype(jnp.float32), xs)
    return from_chunks(o).astype(q.dtype), S


def retention_mixer(h, S0, wq, wk, wv, wg, wo):
    B, T, _ = h.shape
    q = (h @ wq).reshape(B, T, RET_HEADS, RET_DK)
    k = (h @ wk).reshape(B, T, RET_HEADS, RET_DK) * RET_DK ** -0.5
    v = (h @ wv).reshape(B, T, RET_HEADS, RET_DV)
    o, S = retention_recurrence(q, k, v, S0)
    o = head_group_norm(o).reshape(B, T, RET_VALUE_DIM) * jax.nn.silu(h @ wg)
    return o @ wo, S


def alibi_slopes():
    hh = jnp.arange(1, SWA_Q_HEADS + 1, dtype=jnp.float32)
    return (2.0 ** (-8.0 * hh / SWA_Q_HEADS)).reshape(SWA_KV_HEADS, SWA_GROUP)


def swa_project(h, wqkv, qn, kn):
    B, T, _ = h.shape
    nq = SWA_Q_HEADS * SWA_HEAD_DIM
    nk = SWA_KV_HEADS * SWA_HEAD_DIM
    qkv = h @ wqkv
    q = rms_norm(qkv[..., :nq].reshape(B, T, SWA_KV_HEADS, SWA_GROUP, SWA_HEAD_DIM), qn)
    k = rms_norm(qkv[..., nq:nq + nk].reshape(B, T, SWA_KV_HEADS, SWA_HEAD_DIM), kn)
    v = qkv[..., nq + nk:].reshape(B, T, SWA_KV_HEADS, SWA_HEAD_DIM)
    return q, k, v


def sink_attention(q, k, v, q_pos, k_pos, k_valid, sinks):
    logits = jnp.einsum('bnqkgd,bnskd->bnkgqs', q, k, preferred_element_type=jnp.float32) * SWA_HEAD_DIM ** -0.5
    dist = jnp.abs(q_pos[:, :, None] - k_pos[:, None, :]).astype(jnp.float32)
    logits = logits - alibi_slopes()[None, :, :, None, None] * dist[:, None, None]
    logits = jnp.where(k_valid[:, None, None, None, :], logits, NEG_INF)
    sink = jnp.broadcast_to(sinks.astype(jnp.float32).reshape(SWA_KV_HEADS, SWA_GROUP)[:, :, None, None],
                            logits.shape[:-1] + (1,))
    probs = jax.nn.softmax(jnp.concatenate([logits, sink], axis=-1), axis=-1)[..., :-1]
    return jnp.einsum('bnkgqs,bnskd->bnqkgd', probs.astype(v.dtype), v)


def swa_prompt(h, wqkv, qn, kn, sinks, wo):
    B, T, _ = h.shape
    q, k, v = swa_project(h, wqkv, qn, kn)
    L = CHUNK
    NC = T // L
    P = SWA_CHUNKS

    def band(a):
        ap = jnp.pad(a, ((0, 0), (P * L, 0), (0, 0), (0, 0))).reshape(B, NC + P, L, SWA_KV_HEADS, SWA_HEAD_DIM)
        return jnp.concatenate([ap[:, j:j + NC] for j in range(P + 1)], axis=2)

    c = jnp.arange(NC, dtype=jnp.int32)[:, None]
    q_pos = c * L + jnp.arange(L, dtype=jnp.int32)[None, :]
    k_pos = (c - P) * L + jnp.arange((P + 1) * L, dtype=jnp.int32)[None, :]
    k_valid = k_pos >= 0
    qb = q.reshape(B, NC, L, SWA_KV_HEADS, SWA_GROUP, SWA_HEAD_DIM)
    o = sink_attention(qb, band(k), band(v), q_pos, k_pos, k_valid, sinks)
    return o.reshape(B, T, D_MODEL) @ wo, k[:, -WINDOW:], v[:, -WINDOW:]


def swa_sample(h, cache_k, cache_v, wqkv, qn, kn, sinks, wo):
    B, T, _ = h.shape
    q, k, v = swa_project(h, wqkv, qn, kn)
    W = cache_k.shape[1]
    kk = jnp.concatenate([cache_k.astype(k.dtype), k], axis=1)[:, None]
    vv = jnp.concatenate([cache_v.astype(v.dtype), v], axis=1)[:, None]
    q_pos = (PAST_LEN + jnp.arange(T, dtype=jnp.int32))[None, :]
    k_pos = jnp.concatenate([PAST_LEN - W + jnp.arange(W, dtype=jnp.int32),
                             PAST_LEN + jnp.arange(T, dtype=jnp.int32)])[None, :]
    k_valid = jnp.ones((1, W + T), dtype=bool)
    o = sink_attention(q[:, None], kk, vv, q_pos, k_pos, k_valid, sinks)
    return o.reshape(B, T, D_MODEL) @ wo, k, v


def setup_inputs(seed: int = 0) -> dict:
    key = jax.random.key(seed)
    ks = iter(jax.random.split(key, 48))
    D = D_MODEL

    def nrm(shape, scale=1.0):
        return jax.random.normal(next(ks), shape, jnp.float32) * scale

    qkv_width = (SWA_Q_HEADS + 2 * SWA_KV_HEADS) * SWA_HEAD_DIM
    return {
        'x_prompt': nrm((BATCH, SEQ, D)),
        'x_sample': nrm((DEC_BATCH, DEC_SEQ, D)),
        'state_gla_l0': nrm((DEC_BATCH, GLA_HEADS, GLA_DK, GLA_DV)),
        'state_ret_l1': nrm((DEC_BATCH, RET_HEADS, RET_DK, RET_DV)),
        'cache_swa_k_l2': nrm((DEC_BATCH, SWA_CACHE_LEN, SWA_KV_HEADS, SWA_HEAD_DIM)),
        'cache_swa_v_l2': nrm((DEC_BATCH, SWA_CACHE_LEN, SWA_KV_HEADS, SWA_HEAD_DIM)),
        'state_gla_l3': nrm((DEC_BATCH, GLA_HEADS, GLA_DK, GLA_DV)),
        'c_prompt': nrm((BATCH, D)),
        'c_sample': nrm((DEC_BATCH, D)),
        'norm_mix': 1.0 + nrm((DEPTH, D), 0.05),
        'norm_mlp': 1.0 + nrm((DEPTH, D), 0.05),
        'ada_w': nrm((DEPTH, D, ADA_CHUNKS * D), 0.5 * D ** -0.5),
        'ada_b': nrm((DEPTH, ADA_CHUNKS * D), 0.02),
        'mlp_w1': nrm((DEPTH, D, MLP_HIDDEN), D ** -0.5),
        'mlp_w2': nrm((DEPTH, MLP_HIDDEN, D), MLP_HIDDEN ** -0.5),
        'gla_wq': nrm((N_GLA_LAYERS, D, GLA_KEY_DIM), D ** -0.5),
        'gla_wk': nrm((N_GLA_LAYERS, D, GLA_KEY_DIM), D ** -0.5),
        'gla_wv': nrm((N_GLA_LAYERS, D, D), D ** -0.5),
        'gla_wg1': nrm((N_GLA_LAYERS, D, GLA_GATE_RANK), D ** -0.5),
        'gla_wg2': nrm((N_GLA_LAYERS, GLA_GATE_RANK, GLA_KEY_DIM), GLA_GATE_RANK ** -0.5),
        'gla_bg': nrm((N_GLA_LAYERS, GLA_KEY_DIM), 0.1),
        'gla_wr': nrm((N_GLA_LAYERS, D, D), D ** -0.5),
        'gla_norm': 1.0 + nrm((N_GLA_LAYERS, GLA_DV), 0.05),
        'gla_wo': nrm((N_GLA_LAYERS, D, D), D ** -0.5),
        'ret_wq': nrm((N_RET_LAYERS, D, D), D ** -0.5),
        'ret_wk': nrm((N_RET_LAYERS, D, D), D ** -0.5),
        'ret_wv': nrm((N_RET_LAYERS, D, RET_VALUE_DIM), D ** -0.5),
        'ret_wg': nrm((N_RET_LAYERS, D, RET_VALUE_DIM), D ** -0.5),
        'ret_wo': nrm((N_RET_LAYERS, RET_VALUE_DIM, D), RET_VALUE_DIM ** -0.5),
        'swa_wqkv': nrm((N_SWA_LAYERS, D, qkv_width), D ** -0.5),
        'swa_qnorm': 1.0 + nrm((N_SWA_LAYERS, SWA_HEAD_DIM), 0.05),
        'swa_knorm': 1.0 + nrm((N_SWA_LAYERS, SWA_HEAD_DIM), 0.05),
        'swa_sinks': nrm((N_SWA_LAYERS, SWA_Q_HEADS), 1.0),
        'swa_wo': nrm((N_SWA_LAYERS, D, D), D ** -0.5),
    }


def reference(x_prompt, x_sample, state_gla_l0, state_ret_l1, cache_swa_k_l2, cache_swa_v_l2, state_gla_l3,
              c_prompt, c_sample, norm_mix, norm_mlp, ada_w, ada_b, mlp_w1, mlp_w2,
              gla_wq, gla_wk, gla_wv, gla_wg1, gla_wg2, gla_bg, gla_wr, gla_norm, gla_wo,
              ret_wq, ret_wk, ret_wv, ret_wg, ret_wo,
              swa_wqkv, swa_qnorm, swa_knorm, swa_sinks, swa_wo):
    gla_states = [state_gla_l0, state_gla_l3]
    ret_states = [state_ret_l1]
    swa_caches = [(cache_swa_k_l2, cache_swa_v_l2)]
    yp, ys = x_prompt, x_sample
    bp = x_prompt.shape[0]
    new_state = []
    for layer in range(DEPTH):
        kind, j = layer % N_MIXERS, layer // N_MIXERS
        sh_mp, sc_mp, g_mp, sh_fp, sc_fp, g_fp = ada_modulation(c_prompt, ada_w[layer], ada_b[layer])
        sh_ms, sc_ms, g_ms, sh_fs, sc_fs, g_fs = ada_modulation(c_sample, ada_w[layer], ada_b[layer])
        hp = modulate(yp, norm_mix[layer], sh_mp, sc_mp)
        hs = modulate(ys, norm_mix[layer], sh_ms, sc_ms)
        if kind == 0:
            w = (gla_wq[j], gla_wk[j], gla_wv[j], gla_wg1[j], gla_wg2[j], gla_bg[j], gla_wr[j], gla_norm[j], gla_wo[j])
            zero_state = jnp.zeros((bp, GLA_HEADS, GLA_DK, GLA_DV), jnp.float32)
            o_p, st_p = gla_mixer(hp, zero_state, *w)
            o_s, st_s = gla_mixer(hs, gla_states[j], *w)
            new_state += [st_p, st_s]
        elif kind == 1:
            w = (ret_wq[j], ret_wk[j], ret_wv[j], ret_wg[j], ret_wo[j])
            zero_state = jnp.zeros((bp, RET_HEADS, RET_DK, RET_DV), jnp.float32)
            o_p, st_p = retention_mixer(hp, zero_state, *w)
            o_s, st_s = retention_mixer(hs, ret_states[j], *w)
            new_state += [st_p, st_s]
        else:
            w = (swa_wqkv[j], swa_qnorm[j], swa_knorm[j], swa_sinks[j], swa_wo[j])
            ck, cv = swa_caches[j]
            o_p, k_p, v_p = swa_prompt(hp, *w)
            o_s, k_s, v_s = swa_sample(hs, ck, cv, *w)
            new_state += [k_p, v_p, k_s, v_s]
        yp = yp + g_mp * o_p
        ys = ys + g_ms * o_s
        yp = yp + g_fp * sq_relu_mlp(modulate(yp, norm_mlp[layer], sh_fp, sc_fp), mlp_w1[layer], mlp_w2[layer])
        ys = ys + g_fs * sq_relu_mlp(modulate(ys, norm_mlp[layer], sh_fs, sc_fs), mlp_w1[layer], mlp_w2[layer])
    return (yp, ys, *new_state)
```

```python
import functools

import jax
import jax.numpy as jnp
from jax import lax
from jax.experimental import pallas as pl
from jax.experimental.pallas import tpu as pltpu

F32 = jnp.float32
BF16 = jnp.bfloat16

D_MODEL = 1024
DEPTH = 4
CHUNK = 64
N_MIXERS = 3
GLA_HEADS = 4
GLA_DK = 128
GLA_DV = 256
GLA_RANK = 16
GLA_TAU = 16.0
RET_HEADS = 4
RET_DK = 256
RET_DV = 512
SWA_Q_HEADS = 16
SWA_KV_HEADS = 4
SWA_GROUP = 4
SWA_HD = 64
WINDOW = 128
MLP_HIDDEN = 4 * D_MODEL
ADA_CHUNKS = 6
NORM_EPS = 1e-6
GN_EPS = 1e-5
NEG_INF = -1e30

VMEM_LIMIT_BYTES = 56 * 1024 * 1024
LANES = 128


def _dot(a, b):
    return jnp.dot(a, b, preferred_element_type=F32)


def _dot_nt(a, b):
    return lax.dot_general(a, b, (((1,), (1,)), ((), ())), preferred_element_type=F32)


def _dot_tn(a, b):
    k = a.shape[0]
    if k % LANES:
        pad = LANES - k % LANES
        a = jnp.concatenate([a, jnp.zeros((pad, a.shape[1]), a.dtype)], axis=0)
        b = jnp.concatenate([b, jnp.zeros((pad, b.shape[1]), b.dtype)], axis=0)
    return _dot(a.T.astype(BF16), b.astype(BF16))


def _modulated_norm(x, g, shift, scale):
    ms = jnp.mean(x * x, axis=-1, keepdims=True)
    y = x * lax.rsqrt(ms + NORM_EPS) * g
    return y * (1.0 + scale) + shift


def _silu(x):
    return x * (1.0 / (1.0 + jnp.exp(-x)))


def _resident(shape):
    nd = len(shape)
    return pl.BlockSpec(shape, lambda *_: (0,) * nd, pipeline_mode=pl.Buffered(1))


def _params(sem):
    return pltpu.CompilerParams(dimension_semantics=sem, vmem_limit_bytes=VMEM_LIMIT_BYTES)


def _ada_kernel(c_ref, w_ref, b_ref, o_ref):
    c = c_ref[...]
    a = _silu(c).astype(BF16)
    o_ref[...] = _dot(a, w_ref[...].astype(BF16)) + b_ref[...]


def _ada_modulation(c_all, ada_w, ada_b):
    nb = c_all.shape[0]
    width = ADA_CHUNKS * D_MODEL
    tn = 1536
    return pl.pallas_call(
        _ada_kernel,
        grid=(DEPTH, width // tn),
        in_specs=[
            pl.BlockSpec((nb, D_MODEL), lambda l, j: (0, 0)),
            pl.BlockSpec((None, D_MODEL, tn), lambda l, j: (l, 0, j)),
            pl.BlockSpec((None, 1, tn), lambda l, j: (l, 0, j)),
        ],
        out_specs=pl.BlockSpec((None, nb, tn), lambda l, j: (l, 0, j)),
        out_shape=jax.ShapeDtypeStruct((DEPTH, nb, width), F32),
        compiler_params=_params(("arbitrary", "arbitrary")),
        name="ada_modulation",
    )(c_all, ada_w, ada_b.reshape(DEPTH, 1, width))


def _mlp_kernel(x_ref, mod_ref, g_ref, w1_ref, w2_ref, o_ref, *, th):
    x = x_ref[...]
    mod = mod_ref[...]
    h = _modulated_norm(x, g_ref[...], mod[3:4], mod[4:5]).astype(BF16)
    acc = jnp.zeros(x.shape, F32)
    for c in range(MLP_HIDDEN // th):
        a = _dot(h, w1_ref[:, c * th:(c + 1) * th])
        a = jnp.maximum(a, 0.0)
        a = a * a
        acc = acc + _dot(a.astype(BF16), w2_ref[c * th:(c + 1) * th, :])
    o_ref[...] = x + mod[5:6] * acc


def _mlp_layer(x, mod, g, w1, w2, tm):
    B, T, D = x.shape
    return pl.pallas_call(
        functools.partial(_mlp_kernel, th=512),
        grid=(B, T // tm),
        in_specs=[
            pl.BlockSpec((None, tm, D), lambda b, t: (b, t, 0)),
            pl.BlockSpec((None, ADA_CHUNKS, D), lambda b, t: (b, 0, 0)),
            _resident((1, D)),
            _resident((D, MLP_HIDDEN)),
            _resident((MLP_HIDDEN, D)),
        ],
        out_specs=pl.BlockSpec((None, tm, D), lambda b, t: (b, t, 0)),
        out_shape=jax.ShapeDtypeStruct((B, T, D), F32),
        compiler_params=_params(("arbitrary", "arbitrary")),
        name="mlp_layer",
    )(x, mod, g, w1, w2)


def _chunk_cumsum(a, L):
    row = lax.broadcasted_iota(jnp.int32, a.shape, 0) % L
    s = 1
    while s < L:
        a = a + jnp.where(row >= s, pltpu.roll(a, s, axis=0), 0.0)
        s *= 2
    return a


def _gla_kernel(*refs, L, has_state):
    if has_state:
        (x_ref, mod_ref, ng_ref, wq_ref, wk_ref, wv_ref, wg1_ref, wg2_ref, bg_ref, wr_ref, gn_ref, wo_ref,
         s0_ref, y_ref, s_ref) = refs
    else:
        (x_ref, mod_ref, ng_ref, wq_ref, wk_ref, wv_ref, wg1_ref, wg2_ref, bg_ref, wr_ref, gn_ref, wo_ref,
         y_ref, s_ref) = refs

    @pl.when(pl.program_id(1) == 0)
    def _():
        if has_state:
            s_ref[...] = s0_ref[...]
        else:
            s_ref[...] = jnp.zeros(s_ref.shape, F32)

    x = x_ref[...]
    tm = x.shape[0]
    mod = mod_ref[...]
    hb = _modulated_norm(x, ng_ref[...], mod[0:1], mod[1:2]).astype(BF16)
    q = _dot(hb, wq_ref[...]) * (GLA_DK ** -0.5)
    k = _dot(hb, wk_ref[...])
    v = _dot(hb, wv_ref[...])
    r = _dot(hb, wr_ref[...])
    z = _dot(_dot(hb, wg1_ref[...]).astype(BF16), wg2_ref[...]) + bg_ref[...]
    logg = (jnp.minimum(z, 0.0) - jnp.log1p(jnp.exp(-jnp.abs(z)))) * (1.0 / GLA_TAU)
    b = _chunk_cumsum(logg, L)
    eb = jnp.exp(b)
    qd = (q * eb).astype(BF16)
    kd = (k * jnp.exp(-b)).astype(BF16)
    vb = v.astype(BF16)
    causal = lax.broadcasted_iota(jnp.int32, (L, L), 0) >= lax.broadcasted_iota(jnp.int32, (L, L), 1)
    gn = gn_ref[...]

    o_rows = []
    for c in range(tm // L):
        rows = slice(c * L, (c + 1) * L)
        o_heads = []
        for h in range(GLA_HEADS):
            kc = slice(h * GLA_DK, (h + 1) * GLA_DK)
            vc = slice(h * GLA_DV, (h + 1) * GLA_DV)
            bs = b[rows, kc]
            bL = bs[L - 1:L, :]
            kdec = k[rows, kc] * jnp.exp(bL - bs)
            att = jnp.where(causal, _dot_nt(qd[rows, kc], kd[rows, kc]), 0.0)
            S = s_ref[h]
            o = _dot(att.astype(BF16), vb[rows, vc]) + _dot(qd[rows, kc], S.astype(BF16))
            dcol = jnp.broadcast_to(jnp.exp(bL), (GLA_DK, GLA_DK)).T
            s_ref[h] = S * jnp.concatenate([dcol, dcol], axis=1) + _dot_tn(kdec, vb[rows, vc])
            o = o * lax.rsqrt(jnp.mean(o * o, axis=-1, keepdims=True) + NORM_EPS) * gn
            o_heads.append(o)
        o_rows.append(jnp.concatenate(o_heads, axis=1))
    o = o_rows[0] if len(o_rows) == 1 else jnp.concatenate(o_rows, axis=0)
    o = (o * _silu(r)).astype(BF16)
    y_ref[...] = x + mod[2:3] * _dot(o, wo_ref[...])


def _gla_layer(x, mod, ng, w, s0, tm, L):
    B, T, D = x.shape
    has_state = s0 is not None
    wq, wk, wv, wg1, wg2, bg, wr, gn, wo = w
    state_spec = pl.BlockSpec((None, GLA_HEADS, GLA_DK, GLA_DV), lambda b, t: (b, 0, 0, 0))
    in_specs = [
        pl.BlockSpec((None, tm, D), lambda b, t: (b, t, 0)),
        pl.BlockSpec((None, ADA_CHUNKS, D), lambda b, t: (b, 0, 0)),
        _resident((1, D)),
        _resident(wq.shape), _resident(wk.shape), _resident(wv.shape), _resident(wg1.shape),
        _resident(wg2.shape), _resident(bg.shape), _resident(wr.shape), _resident(gn.shape), _resident(wo.shape),
    ]
    args = [x, mod, ng, wq, wk, wv, wg1, wg2, bg, wr, gn, wo]
    if has_state:
        in_specs.append(state_spec)
        args.append(s0)
    return pl.pallas_call(
        functools.partial(_gla_kernel, L=L, has_state=has_state),
        grid=(B, T // tm),
        in_specs=in_specs,
        out_specs=[pl.BlockSpec((None, tm, D), lambda b, t: (b, t, 0)), state_spec],
        out_shape=[jax.ShapeDtypeStruct((B, T, D), F32),
                   jax.ShapeDtypeStruct((B, GLA_HEADS, GLA_DK, GLA_DV), F32)],
        compiler_params=_params(("arbitrary", "arbitrary")),
        name="gla_layer",
    )(*args)


def _ret_kernel(*refs, L, has_state):
    if has_state:
        x_ref, mod_ref, ng_ref, wq_ref, wk_ref, wv_ref, wg_ref, wo_ref, s0_ref, y_ref, s_ref = refs
    else:
        x_ref, mod_ref, ng_ref, wq_ref, wk_ref, wv_ref, wg_ref, wo_ref, y_ref, s_ref = refs

    @pl.when(pl.program_id(1) == 0)
    def _():
        if has_state:
            s_ref[...] = s0_ref[...]
        else:
            s_ref[...] = jnp.zeros(s_ref.shape, F32)

    x = x_ref[...]
    tm = x.shape[0]
    mod = mod_ref[...]
    hb = _modulated_norm(x, ng_ref[...], mod[0:1], mod[1:2]).astype(BF16)
    q = _dot(hb, wq_ref[...])
    k = _dot(hb, wk_ref[...]) * (RET_DK ** -0.5)
    v = _dot(hb, wv_ref[...])
    g = _dot(hb, wg_ref[...])
    qb = q.astype(BF16)
    vb = v.astype(BF16)

    rel = (lax.broadcasted_iota(jnp.int32, (L, L), 0) - lax.broadcasted_iota(jnp.int32, (L, L), 1)).astype(F32)
    pos_q = lax.broadcasted_iota(jnp.int32, (L, RET_DV), 0).astype(F32)
    pos_k = lax.broadcasted_iota(jnp.int32, (L, RET_DK), 0).astype(F32)

    o_heads = []
    for h in range(RET_HEADS):
        lg = jnp.log(jnp.float32(1.0 - 2.0 ** (-5.0 - h)))
        decay_intra = jnp.where(rel >= 0.0, jnp.exp(lg * jnp.maximum(rel, 0.0)), 0.0)
        decay_query = jnp.exp(lg * (pos_q + 1.0))
        decay_key = jnp.exp(lg * (L - 1.0 - pos_k))
        decay_chunk = jnp.exp(lg * L)
        kc = slice(h * RET_DK, (h + 1) * RET_DK)
        vc = slice(h * RET_DV, (h + 1) * RET_DV)
        o_rows = []
        for c in range(tm // L):
            rows = slice(c * L, (c + 1) * L)
            att = _dot_nt(qb[rows, kc], k[rows, kc].astype(BF16)) * decay_intra
            S = s_ref[h]
            o = _dot(att.astype(BF16), vb[rows, vc]) + _dot(qb[rows, kc], S.astype(BF16)) * decay_query
            s_ref[h] = S * decay_chunk + _dot_tn(k[rows, kc] * decay_key, vb[rows, vc])
            mu = jnp.mean(o, axis=-1, keepdims=True)
            oc = o - mu
            o_rows.append(oc * lax.rsqrt(jnp.mean(oc * oc, axis=-1, keepdims=True) + GN_EPS))
        o_heads.append(o_rows[0] if len(o_rows) == 1 else jnp.concatenate(o_rows, axis=0))
    o = jnp.concatenate(o_heads, axis=1)
    o = (o * _silu(g)).astype(BF16)
    y_ref[...] = x + mod[2:3] * _dot(o, wo_ref[...])


def _ret_layer(x, mod, ng, w, s0, tm, L):
    B, T, D = x.shape
    has_state = s0 is not None
    wq, wk, wv, wg, wo = w
    state_spec = pl.BlockSpec((None, RET_HEADS, RET_DK, RET_DV), lambda b, t: (b, 0, 0, 0))
    in_specs = [
        pl.BlockSpec((None, tm, D), lambda b, t: (b, t, 0)),
        pl.BlockSpec((None, ADA_CHUNKS, D), lambda b, t: (b, 0, 0)),
        _resident((1, D)),
        _resident(wq.shape), _resident(wk.shape), _resident(wv.shape), _resident(wg.shape), _resident(wo.shape),
    ]
    args = [x, mod, ng, wq, wk, wv, wg, wo]
    if has_state:
        in_specs.append(state_spec)
        args.append(s0)
    return pl.pallas_call(
        functools.partial(_ret_kernel, L=L, has_state=has_state),
        grid=(B, T // tm),
        in_specs=in_specs,
        out_specs=[pl.BlockSpec((None, tm, D), lambda b, t: (b, t, 0)), state_spec],
        out_shape=[jax.ShapeDtypeStruct((B, T, D), F32),
                   jax.ShapeDtypeStruct((B, RET_HEADS, RET_DK, RET_DV), F32)],
        compiler_params=_params(("arbitrary", "arbitrary")),
        name="ret_layer",
    )(*args)


def _head_rms(a, g):
    out = []
    for h in range(a.shape[1] // SWA_HD):
        ah = a[:, h * SWA_HD:(h + 1) * SWA_HD]
        out.append(ah * lax.rsqrt(jnp.mean(ah * ah, axis=-1, keepdims=True) + NORM_EPS) * g)
    return out


def _swa_kernel(*refs, Lq, is_prompt):
    if is_prompt:
        (sink_ref, x_ref, mod_ref, ng_ref, wqkv_ref, qn_ref, kn_ref, wo_ref, y_ref, kc_ref, vc_ref) = refs
    else:
        (sink_ref, x_ref, mod_ref, ng_ref, wqkv_ref, qn_ref, kn_ref, wo_ref, ck_ref, cv_ref,
         y_ref, kc_ref, vc_ref) = refs
    t = pl.program_id(1)
    x = x_ref[...]
    tm = x.shape[0]
    nq = SWA_Q_HEADS * SWA_HD
    nk = SWA_KV_HEADS * SWA_HD
    mod = mod_ref[...]
    hb = _modulated_norm(x, ng_ref[...], mod[0:1], mod[1:2]).astype(BF16)
    qkv = _dot(hb, wqkv_ref[...])
    qh = _head_rms(qkv[:, :nq], qn_ref[...])
    kh = _head_rms(qkv[:, nq:nq + nk], kn_ref[...])
    k_new = jnp.concatenate(kh, axis=1)
    v_new = qkv[:, nq + nk:]

    if is_prompt:
        @pl.when(t == 0)
        def _():
            kc_ref[...] = jnp.zeros(kc_ref.shape, F32)
            vc_ref[...] = jnp.zeros(vc_ref.shape, F32)
        k_prev = kc_ref[...]
        v_prev = vc_ref[...]
    else:
        k_prev = ck_ref[...]
        v_prev = cv_ref[...]
    k_ext = jnp.concatenate([k_prev, k_new], axis=0).astype(BF16)
    v_ext = jnp.concatenate([v_prev, v_new], axis=0).astype(BF16)

    nkeys = WINDOW + Lq
    rows_q = SWA_GROUP * Lq
    ri = lax.broadcasted_iota(jnp.int32, (rows_q, nkeys), 0)
    ji = lax.broadcasted_iota(jnp.int32, (rows_q, nkeys), 1)
    grp = ri // Lq
    dist = jnp.abs((ri - grp * Lq) - ji + WINDOW).astype(F32)
    grp_col = lax.broadcasted_iota(jnp.int32, (rows_q, 1), 0) // Lq

    o_rows = [[None] * SWA_Q_HEADS for _ in range(tm // Lq)]
    for kv in range(SWA_KV_HEADS):
        slope = jnp.exp2(-0.5 * (kv * SWA_GROUP + grp + 1).astype(F32))
        bias = -slope * dist
        sink = jnp.zeros((rows_q, 1), F32)
        for gi in range(SWA_GROUP):
            sink = jnp.where(grp_col == gi, sink_ref[kv * SWA_GROUP + gi], sink)
        hc = slice(kv * SWA_HD, (kv + 1) * SWA_HD)
        for c in range(tm // Lq):
            r0 = c * Lq
            qs = jnp.concatenate([qh[kv * SWA_GROUP + gi][r0:r0 + Lq] for gi in range(SWA_GROUP)], axis=0)
            kx = k_ext[r0:r0 + nkeys, hc]
            vx = v_ext[r0:r0 + nkeys, hc]
            logits = _dot_nt(qs.astype(BF16), kx) * (SWA_HD ** -0.5) + bias
            if is_prompt:
                logits = jnp.where(ji >= WINDOW - r0 - t * tm, logits, NEG_INF)
            m = jnp.maximum(jnp.max(logits, axis=-1, keepdims=True), sink)
            p = jnp.exp(logits - m)
            den = jnp.sum(p, axis=-1, keepdims=True) + jnp.exp(sink - m)
            oc = _dot(p.astype(BF16), vx) * (1.0 / den)
            for gi in range(SWA_GROUP):
                o_rows[c][kv * SWA_GROUP + gi] = oc[gi * Lq:(gi + 1) * Lq]
    o = [jnp.concatenate(r, axis=1) for r in o_rows]
    o = o[0] if len(o) == 1 else jnp.concatenate(o, axis=0)
    y_ref[...] = x + mod[2:3] * _dot(o.astype(BF16), wo_ref[...])
    if is_prompt:
        kc_ref[...] = k_new[tm - WINDOW:]
        vc_ref[...] = v_new[tm - WINDOW:]
    else:
        kc_ref[...] = k_new
        vc_ref[...] = v_new


def _swa_layer(x, mod, ng, w, cache, tm, Lq):
    B, T, D = x.shape
    is_prompt = cache is None
    wqkv, qn, kn, sinks, wo = w
    nk = SWA_KV_HEADS * SWA_HD
    in_specs = [
        pl.BlockSpec(memory_space=pltpu.SMEM),
        pl.BlockSpec((None, tm, D), lambda b, t: (b, t, 0)),
        pl.BlockSpec((None, ADA_CHUNKS, D), lambda b, t: (b, 0, 0)),
        _resident((1, D)),
        _resident(wqkv.shape), _resident(qn.shape), _resident(kn.shape), _resident(wo.shape),
    ]
    args = [sinks, x, mod, ng, wqkv, qn, kn, wo]
    if is_prompt:
        rows_out = WINDOW
    else:
        ck, cv = cache
        in_specs += [pl.BlockSpec((None, WINDOW, nk), lambda b, t: (b, 0, 0))] * 2
        args += [ck.reshape(B, WINDOW, nk), cv.reshape(B, WINDOW, nk)]
        rows_out = T
    cache_spec = pl.BlockSpec((None, rows_out, nk), lambda b, t: (b, 0, 0))
    y, kc, vc = pl.pallas_call(
        functools.partial(_swa_kernel, Lq=Lq, is_prompt=is_prompt),
        grid=(B, T // tm),
        in_specs=in_specs,
        out_specs=[pl.BlockSpec((None, tm, D), lambda b, t: (b, t, 0)), cache_spec, cache_spec],
        out_shape=[jax.ShapeDtypeStruct((B, T, D), F32),
                   jax.ShapeDtypeStruct((B, rows_out, nk), F32),
                   jax.ShapeDtypeStruct((B, rows_out, nk), F32)],
        compiler_params=_params(("arbitrary", "arbitrary")),
        name="swa_layer",
    )(*args)
    return (y, kc.reshape(B, rows_out, SWA_KV_HEADS, SWA_HD), vc.reshape(B, rows_out, SWA_KV_HEADS, SWA_HD))


def _tile(T, pref):
    return pref if T % pref == 0 else T


def kernel(x_prompt, x_sample, state_gla_l0, state_ret_l1, cache_swa_k_l2, cache_swa_v_l2, state_gla_l3, c_prompt, c_sample, norm_mix, norm_mlp, ada_w, ada_b, mlp_w1, mlp_w2, gla_wq, gla_wk, gla_wv, gla_wg1, gla_wg2, gla_bg, gla_wr, gla_norm, gla_wo, ret_wq, ret_wk, ret_wv, ret_wg, ret_wo, swa_wqkv, swa_qnorm, swa_knorm, swa_sinks, swa_wo):
    bp = x_prompt.shape[0]
    bs = x_sample.shape[0]
    Tp = x_prompt.shape[1]
    Ts = x_sample.shape[1]
    gla_states = [state_gla_l0, state_gla_l3]
    ret_states = [state_ret_l1]
    swa_caches = [(cache_swa_k_l2, cache_swa_v_l2)]

    mod_all = _ada_modulation(jnp.concatenate([c_prompt, c_sample], axis=0), ada_w, ada_b)
    mod_all = mod_all.reshape(DEPTH, bp + bs, ADA_CHUNKS, D_MODEL)

    yp, ys = x_prompt, x_sample
    new_state = []
    for layer in range(DEPTH):
        kind, j = layer % N_MIXERS, layer // N_MIXERS
        mod_p = mod_all[layer, :bp]
        mod_s = mod_all[layer, bp:]
        ng = norm_mix[layer].reshape(1, D_MODEL)
        if kind == 0:
            w = (gla_wq[j].astype(BF16), gla_wk[j].astype(BF16), gla_wv[j].astype(BF16), gla_wg1[j].astype(BF16),
                 gla_wg2[j].astype(BF16), gla_bg[j].reshape(1, -1), gla_wr[j].astype(BF16),
                 gla_norm[j].reshape(1, -1), gla_wo[j].astype(BF16))
            yp, st_p = _gla_layer(yp, mod_p, ng, w, None, _tile(Tp, 256), min(CHUNK, Tp))
            ys, st_s = _gla_layer(ys, mod_s, ng, w, gla_states[j], Ts, min(CHUNK, Ts))
            new_state += [st_p, st_s]
        elif kind == 1:
            w = (ret_wq[j].astype(BF16), ret_wk[j].astype(BF16), ret_wv[j].astype(BF16), ret_wg[j].astype(BF16),
                 ret_wo[j].astype(BF16))
            yp, st_p = _ret_layer(yp, mod_p, ng, w, None, _tile(Tp, 256), min(2 * CHUNK, Tp))
            ys, st_s = _ret_layer(ys, mod_s, ng, w, ret_states[j], Ts, min(CHUNK, Ts))
            new_state += [st_p, st_s]
        else:
            w = (swa_wqkv[j].astype(BF16), swa_qnorm[j].reshape(1, -1), swa_knorm[j].reshape(1, -1),
                 swa_sinks[j], swa_wo[j].astype(BF16))
            yp, k_p, v_p = _swa_layer(yp, mod_p, ng, w, None, _tile(Tp, 256), CHUNK)
            ys, k_s, v_s = _swa_layer(ys, mod_s, ng, w, swa_caches[j], Ts, Ts)
            new_state += [k_p, v_p, k_s, v_s]
        gm = norm_mlp[layer].reshape(1, D_MODEL)
        w1 = mlp_w1[layer].astype(BF16)
        w2 = mlp_w2[layer].astype(BF16)
        yp = _mlp_layer(yp, mod_p, gm, w1, w2, _tile(Tp, 512))
        ys = _mlp_layer(ys, mod_s, gm, w1, w2, Ts)
    return (yp, ys, *new_state)
```

```python
import functools
import math

import jax
import jax.numpy as jnp
from jax import lax
from jax.experimental import pallas as pl
from jax.experimental.pallas import tpu as pltpu

F32 = jnp.float32
BF16 = jnp.bfloat16

D_MODEL = 1024
DEPTH = 4
CHUNK = 64
N_MIXERS = 3
GLA_HEADS = 4
GLA_DK = 128
GLA_DV = 256
GLA_RANK = 16
GLA_TAU = 16.0
RET_HEADS = 4
RET_DK = 256
RET_DV = 512
SWA_Q_HEADS = 16
SWA_KV_HEADS = 4
SWA_GROUP = 4
SWA_HD = 64
WINDOW = 128
MLP_HIDDEN = 4 * D_MODEL
ADA_CHUNKS = 6
NORM_EPS = 1e-6
GN_EPS = 1e-5
NEG_INF = -1e30

VMEM_LIMIT_BYTES = 56 * 1024 * 1024
LANES = 128


def _dot(a, b):
    return jnp.dot(a, b, preferred_element_type=F32)


def _dot_nt(a, b):
    return lax.dot_general(a, b, (((1,), (1,)), ((), ())), preferred_element_type=F32)


def _dot_tn(a, b):
    k = a.shape[0]
    if k % LANES:
        pad = LANES - k % LANES
        a = jnp.concatenate([a, jnp.zeros((pad, a.shape[1]), a.dtype)], axis=0)
        b = jnp.concatenate([b, jnp.zeros((pad, b.shape[1]), b.dtype)], axis=0)
    return _dot(a.T.astype(BF16), b.astype(BF16))


def _pad_rows(a, mult):
    r = (-a.shape[0]) % mult
    return a if r == 0 else jnp.concatenate([a, jnp.zeros((r, a.shape[1]), a.dtype)], axis=0)


def _modulated_norm(x, g, shift, scale):
    ms = jnp.mean(x * x, axis=-1, keepdims=True)
    y = x * lax.rsqrt(ms + NORM_EPS) * g
    return y * (1.0 + scale) + shift


def _silu(x):
    return x * (1.0 / (1.0 + jnp.exp(-x)))


def _resident(shape):
    nd = len(shape)
    return pl.BlockSpec(shape, lambda *_: (0,) * nd, pipeline_mode=pl.Buffered(1))


def _params(sem):
    return pltpu.CompilerParams(dimension_semantics=sem, vmem_limit_bytes=VMEM_LIMIT_BYTES)


def _ada_kernel(c_ref, w_ref, b_ref, o_ref):
    c = c_ref[...]
    a = _silu(c).astype(BF16)
    o_ref[...] = _dot(a, w_ref[...].astype(BF16)) + b_ref[...]


def _ada_modulation(c_all, ada_w, ada_b):
    nb = c_all.shape[0]
    width = ADA_CHUNKS * D_MODEL
    tn = 1536
    return pl.pallas_call(
        _ada_kernel,
        grid=(DEPTH, width // tn),
        in_specs=[
            pl.BlockSpec((nb, D_MODEL), lambda l, j: (0, 0)),
            pl.BlockSpec((None, D_MODEL, tn), lambda l, j: (l, 0, j)),
            pl.BlockSpec((None, 1, tn), lambda l, j: (l, 0, j)),
        ],
        out_specs=pl.BlockSpec((None, nb, tn), lambda l, j: (l, 0, j)),
        out_shape=jax.ShapeDtypeStruct((DEPTH, nb, width), F32),
        compiler_params=_params(("arbitrary", "arbitrary")),
        name="ada_modulation",
    )(c_all, ada_w, ada_b.reshape(DEPTH, 1, width))


def _mlp_kernel(x_ref, mod_ref, g_ref, w1_ref, w2_ref, o_ref, *, th):
    x = x_ref[...]
    mod = mod_ref[...]
    h = _modulated_norm(x, g_ref[...], mod[3:4], mod[4:5]).astype(BF16)
    acc = jnp.zeros(x.shape, F32)
    for c in range(MLP_HIDDEN // th):
        a = _dot(h, w1_ref[:, c * th:(c + 1) * th])
        a = jnp.maximum(a, 0.0)
        a = a * a
        acc = acc + _dot(a.astype(BF16), w2_ref[c * th:(c + 1) * th, :])
    o_ref[...] = x + mod[5:6] * acc


def _mlp_layer(x, mod, g, w1, w2, tm):
    B, T, D = x.shape
    return pl.pallas_call(
        functools.partial(_mlp_kernel, th=512),
        grid=(B, T // tm),
        in_specs=[
            pl.BlockSpec((None, tm, D), lambda b, t: (b, t, 0)),
            pl.BlockSpec((None, ADA_CHUNKS, D), lambda b, t: (b, 0, 0)),
            _resident((1, D)),
            _resident((D, MLP_HIDDEN)),
            _resident((MLP_HIDDEN, D)),
        ],
        out_specs=pl.BlockSpec((None, tm, D), lambda b, t: (b, t, 0)),
        out_shape=jax.ShapeDtypeStruct((B, T, D), F32),
        compiler_params=_params(("arbitrary", "arbitrary")),
        name="mlp_layer",
    )(x, mod, g, w1, w2)


def _chunk_cumsum(a, L):
    row = lax.broadcasted_iota(jnp.int32, a.shape, 0) % L
    s = 1
    while s < L:
        a = a + jnp.where(row >= s, pltpu.roll(a, s, axis=0), 0.0)
        s *= 2
    return a


def _gla_kernel(*refs, L, has_state):
    if has_state:
        (x_ref, mod_ref, ng_ref, wq_ref, wk_ref, wv_ref, wg1_ref, wg2_ref, bg_ref, wr_ref, gn_ref, wo_ref,
         s0_ref, y_ref, s_ref) = refs
    else:
        (x_ref, mod_ref, ng_ref, wq_ref, wk_ref, wv_ref, wg1_ref, wg2_ref, bg_ref, wr_ref, gn_ref, wo_ref,
         y_ref, s_ref) = refs

    @pl.when(pl.program_id(1) == 0)
    def _():
        if has_state:
            s_ref[...] = s0_ref[...]
        else:
            s_ref[...] = jnp.zeros(s_ref.shape, F32)

    x = x_ref[...]
    tm = x.shape[0]
    mod = mod_ref[...]
    hb = _modulated_norm(x, ng_ref[...], mod[0:1], mod[1:2]).astype(BF16)
    q = _dot(hb, wq_ref[...]) * (GLA_DK ** -0.5)
    k = _dot(hb, wk_ref[...])
    v = _dot(hb, wv_ref[...])
    r = _dot(hb, wr_ref[...])
    z = _dot(_dot(hb, wg1_ref[...]).astype(BF16), wg2_ref[...]) + bg_ref[...]
    logg = (jnp.minimum(z, 0.0) - jnp.log(1.0 + jnp.exp(-jnp.abs(z)))) * (1.0 / GLA_TAU)
    nc = tm // L
    b = _chunk_cumsum(logg, L)
    qd = (q * jnp.exp(b)).astype(BF16)
    kdf = k * jnp.exp(-b)
    kd = kdf.astype(BF16)
    vb = _pad_rows(v.astype(BF16), LANES)
    ebl = [jnp.exp(b[c * L + L - 1:c * L + L, :]) for c in range(nc)]
    ebl_rows = [jnp.broadcast_to(e, (L, e.shape[1])) for e in ebl]
    kdec = kdf * (ebl_rows[0] if nc == 1 else jnp.concatenate(ebl_rows, axis=0))
    ri = lax.broadcasted_iota(jnp.int32, (tm, tm), 0)
    ci = lax.broadcasted_iota(jnp.int32, (tm, tm), 1)
    mask = ri >= ci
    if nc > 1:
        mask = mask & (ci >= (ri // L) * L)
    lane_chunk = lax.broadcasted_iota(jnp.int32, (GLA_DK, vb.shape[0]), 1) // L
    gn = gn_ref[...]

    o_heads = []
    for h in range(GLA_HEADS):
        kc = slice(h * GLA_DK, (h + 1) * GLA_DK)
        vh = vb[:, h * GLA_DV:(h + 1) * GLA_DV]
        att = jnp.where(mask, _dot_nt(qd[:, kc], kd[:, kc]), 0.0).astype(BF16)
        o = _dot(att, vh[:tm])
        kdec_t = _pad_rows(kdec[:, kc], LANES).T
        lhs = [jnp.where(lane_chunk == c, kdec_t, 0.0) for c in range(nc)]
        lhs = (lhs[0] if nc == 1 else jnp.concatenate(lhs, axis=0)).astype(BF16)
        u = _dot(lhs, vh)
        S = s_ref[h]
        o_inter = []
        for c in range(nc):
            o_inter.append(_dot(qd[c * L:(c + 1) * L, kc], S.astype(BF16)))
            dcol = jnp.broadcast_to(ebl[c][:, kc], (GLA_DK, GLA_DK)).T
            S = S * jnp.concatenate([dcol, dcol], axis=1) + u[c * GLA_DK:(c + 1) * GLA_DK]
        s_ref[h] = S
        o = o + (o_inter[0] if nc == 1 else jnp.concatenate(o_inter, axis=0))
        o_heads.append(o * lax.rsqrt(jnp.mean(o * o, axis=-1, keepdims=True) + NORM_EPS) * gn)
    o = jnp.concatenate(o_heads, axis=1)
    o = (o * _silu(r)).astype(BF16)
    y_ref[...] = x + mod[2:3] * _dot(o, wo_ref[...])


def _gla_layer(x, mod, ng, w, s0, tm, L):
    B, T, D = x.shape
    has_state = s0 is not None
    wq, wk, wv, wg1, wg2, bg, wr, gn, wo = w
    state_spec = pl.BlockSpec((None, GLA_HEADS, GLA_DK, GLA_DV), lambda b, t: (b, 0, 0, 0))
    in_specs = [
        pl.BlockSpec((None, tm, D), lambda b, t: (b, t, 0)),
        pl.BlockSpec((None, ADA_CHUNKS, D), lambda b, t: (b, 0, 0)),
        _resident((1, D)),
        _resident(wq.shape), _resident(wk.shape), _resident(wv.shape), _resident(wg1.shape),
        _resident(wg2.shape), _resident(bg.shape), _resident(wr.shape), _resident(gn.shape), _resident(wo.shape),
    ]
    args = [x, mod, ng, wq, wk, wv, wg1, wg2, bg, wr, gn, wo]
    if has_state:
        in_specs.append(state_spec)
        args.append(s0)
    return pl.pallas_call(
        functools.partial(_gla_kernel, L=L, has_state=has_state),
        grid=(B, T // tm),
        in_specs=in_specs,
        out_specs=[pl.BlockSpec((None, tm, D), lambda b, t: (b, t, 0)), state_spec],
        out_shape=[jax.ShapeDtypeStruct((B, T, D), F32),
                   jax.ShapeDtypeStruct((B, GLA_HEADS, GLA_DK, GLA_DV), F32)],
        compiler_params=_params(("arbitrary", "arbitrary")),
        name="gla_layer",
    )(*args)


def _ret_kernel(*refs, has_state):
    if has_state:
        (x_ref, mod_ref, ng_ref, wq_ref, wk_ref, wv_ref, wg_ref, wo_ref, s0_ref, y_ref, s_ref,
         dintra_ref, dquery_ref, dkey_ref) = refs
    else:
        (x_ref, mod_ref, ng_ref, wq_ref, wk_ref, wv_ref, wg_ref, wo_ref, y_ref, s_ref,
         dintra_ref, dquery_ref, dkey_ref) = refs
    L = x_ref.shape[0]
    log_decay = [math.log(1.0 - 2.0 ** (-5.0 - h)) for h in range(RET_HEADS)]

    @pl.when((pl.program_id(0) == 0) & (pl.program_id(1) == 0))
    def _():
        rel = (lax.broadcasted_iota(jnp.int32, (L, L), 0) - lax.broadcasted_iota(jnp.int32, (L, L), 1)).astype(F32)
        pos = lax.broadcasted_iota(jnp.int32, (L, LANES), 0).astype(F32)
        for h in range(RET_HEADS):
            dintra_ref[h] = jnp.where(rel >= 0.0, jnp.exp(log_decay[h] * jnp.maximum(rel, 0.0)), 0.0)
            dquery_ref[h] = jnp.exp(log_decay[h] * (pos + 1.0))
            dkey_ref[h] = jnp.exp(log_decay[h] * (L - 1.0 - pos))

    @pl.when(pl.program_id(1) == 0)
    def _():
        if has_state:
            s_ref[...] = s0_ref[...]
        else:
            s_ref[...] = jnp.zeros(s_ref.shape, F32)

    x = x_ref[...]
    mod = mod_ref[...]
    hb = _modulated_norm(x, ng_ref[...], mod[0:1], mod[1:2]).astype(BF16)
    q = _dot(hb, wq_ref[...])
    k = _dot(hb, wk_ref[...]) * (RET_DK ** -0.5)
    v = _dot(hb, wv_ref[...])
    g = _dot(hb, wg_ref[...])
    qb = q.astype(BF16)
    kb = k.astype(BF16)
    vb = _pad_rows(v.astype(BF16), LANES)

    o_heads = []
    for h in range(RET_HEADS):
        kc = slice(h * RET_DK, (h + 1) * RET_DK)
        vh = vb[:, h * RET_DV:(h + 1) * RET_DV]
        dq = dquery_ref[h]
        dk = dkey_ref[h]
        att = (_dot_nt(qb[:, kc], kb[:, kc]) * dintra_ref[h]).astype(BF16)
        S = s_ref[h]
        o = _dot(att, vh[:L]) + _dot(qb[:, kc], S.astype(BF16)) * jnp.concatenate([dq] * (RET_DV // LANES), axis=1)
        k_dec = _pad_rows(k[:, kc] * jnp.concatenate([dk] * (RET_DK // LANES), axis=1), LANES)
        s_ref[h] = S * math.exp(log_decay[h] * L) + _dot(k_dec.T.astype(BF16), vh)
        oc = o - jnp.mean(o, axis=-1, keepdims=True)
        o_heads.append(oc * lax.rsqrt(jnp.mean(oc * oc, axis=-1, keepdims=True) + GN_EPS))
    o = jnp.concatenate(o_heads, axis=1)
    o = (o * _silu(g)).astype(BF16)
    y_ref[...] = x + mod[2:3] * _dot(o, wo_ref[...])


def _ret_layer(x, mod, ng, w, s0, tm):
    B, T, D = x.shape
    has_state = s0 is not None
    wq, wk, wv, wg, wo = w
    state_spec = pl.BlockSpec((None, RET_HEADS, RET_DK, RET_DV), lambda b, t: (b, 0, 0, 0))
    in_specs = [
        pl.BlockSpec((None, tm, D), lambda b, t: (b, t, 0)),
        pl.BlockSpec((None, ADA_CHUNKS, D), lambda b, t: (b, 0, 0)),
        _resident((1, D)),
        _resident(wq.shape), _resident(wk.shape), _resident(wv.shape), _resident(wg.shape), _resident(wo.shape),
    ]
    args = [x, mod, ng, wq, wk, wv, wg, wo]
    if has_state:
        in_specs.append(state_spec)
        args.append(s0)
    return pl.pallas_call(
        functools.partial(_ret_kernel, has_state=has_state),
        grid=(B, T // tm),
        in_specs=in_specs,
        out_specs=[pl.BlockSpec((None, tm, D), lambda b, t: (b, t, 0)), state_spec],
        out_shape=[jax.ShapeDtypeStruct((B, T, D), F32),
                   jax.ShapeDtypeStruct((B, RET_HEADS, RET_DK, RET_DV), F32)],
        scratch_shapes=[pltpu.VMEM((RET_HEADS, tm, tm), F32), pltpu.VMEM((RET_HEADS, tm, LANES), F32),
                        pltpu.VMEM((RET_HEADS, tm, LANES), F32)],
        compiler_params=_params(("arbitrary", "arbitrary")),
        name="ret_layer",
    )(*args)


def _head_mean_square(a, hsum):
    w = hsum.shape[0]
    parts = [_dot((a[:, j:j + w] * a[:, j:j + w]).astype(BF16), hsum) for j in range(0, a.shape[1], w)]
    return parts[0] if len(parts) == 1 else jnp.concatenate(parts, axis=1)


def _tile_head_lanes(a, kv):
    t = a[:, (kv // 2) * LANES:(kv // 2 + 1) * LANES]
    low = lax.broadcasted_iota(jnp.int32, t.shape, 1) < SWA_HD
    keep = low if kv % 2 == 0 else jnp.logical_not(low)
    t = jnp.where(keep, t, pltpu.roll(t, SWA_HD, axis=1)).astype(BF16)
    return jnp.concatenate([t, t], axis=1)


def _swa_kernel(*refs, Lq, is_prompt):
    if is_prompt:
        (sink_ref, x_ref, mod_ref, ng_ref, wqkv_ref, qn_ref, kn_ref, wo_ref, y_ref, kc_ref, vc_ref,
         bias_ref) = refs
    else:
        (sink_ref, x_ref, mod_ref, ng_ref, wqkv_ref, qn_ref, kn_ref, wo_ref, ck_ref, cv_ref,
         y_ref, kc_ref, vc_ref, bias_ref) = refs
    t = pl.program_id(1)
    x = x_ref[...]
    tm = x.shape[0]
    nq = SWA_Q_HEADS * SWA_HD
    nk = SWA_KV_HEADS * SWA_HD
    gw = SWA_GROUP * SWA_HD
    nkeys = WINDOW + Lq
    rows_q = SWA_GROUP * Lq

    @pl.when((pl.program_id(0) == 0) & (t == 0))
    def _():
        ri = lax.broadcasted_iota(jnp.int32, (rows_q, nkeys), 0)
        ji = lax.broadcasted_iota(jnp.int32, (rows_q, nkeys), 1)
        grp = ri // Lq
        dist = jnp.abs((ri - grp * Lq) - ji + WINDOW).astype(F32)
        for kv in range(SWA_KV_HEADS):
            bias_ref[kv] = -jnp.exp2(-0.5 * (kv * SWA_GROUP + grp + 1).astype(F32)) * dist

    mod = mod_ref[...]
    hb = _modulated_norm(x, ng_ref[...], mod[0:1], mod[1:2]).astype(BF16)
    qkv = _dot(hb, wqkv_ref[...])
    hi = lax.broadcasted_iota(jnp.int32, (gw, gw), 0) // SWA_HD
    hj = lax.broadcasted_iota(jnp.int32, (gw, gw), 1) // SWA_HD
    hsum = jnp.where(hi == hj, 1.0 / SWA_HD, 0.0).astype(BF16)
    q = qkv[:, :nq]
    k = qkv[:, nq:nq + nk]
    q = (q * lax.rsqrt(_head_mean_square(q, hsum) + NORM_EPS) * (qn_ref[...] * SWA_HD ** -0.5)).astype(BF16)
    k_new = k * lax.rsqrt(_head_mean_square(k, hsum) + NORM_EPS) * kn_ref[...]
    v_new = qkv[:, nq + nk:]

    if is_prompt:
        @pl.when(t == 0)
        def _():
            kc_ref[...] = jnp.zeros(kc_ref.shape, F32)
            vc_ref[...] = jnp.zeros(vc_ref.shape, F32)
        k_prev = kc_ref[...]
        v_prev = vc_ref[...]
    else:
        k_prev = ck_ref[...]
        v_prev = cv_ref[...]
    k_ext = jnp.concatenate([k_prev, k_new], axis=0)
    v_ext = jnp.concatenate([v_prev, v_new], axis=0)

    lane_grp = lax.broadcasted_iota(jnp.int32, (Lq, gw), 1) // SWA_HD
    grp_col = lax.broadcasted_iota(jnp.int32, (rows_q, 1), 0) // Lq
    ji = lax.broadcasted_iota(jnp.int32, (rows_q, nkeys), 1)
    zero = jnp.zeros((), BF16)

    o_cols = []
    for kv in range(SWA_KV_HEADS):
        kt = _tile_head_lanes(k_ext, kv)
        vt = _tile_head_lanes(v_ext, kv)
        bias = bias_ref[kv]
        sink = jnp.zeros((rows_q, 1), F32)
        for gi in range(SWA_GROUP):
            sink = jnp.where(grp_col == gi, sink_ref[kv * SWA_GROUP + gi], sink)
        o_rows = []
        for c in range(tm // Lq):
            r0 = c * Lq
            qc = q[r0:r0 + Lq, kv * gw:(kv + 1) * gw]
            qs = jnp.concatenate([jnp.where(lane_grp == gi, qc, zero) for gi in range(SWA_GROUP)], axis=0)
            logits = _dot_nt(qs, kt[r0:r0 + nkeys]) + bias
            if is_prompt and r0 < WINDOW:
                logits = jnp.where(ji >= WINDOW - r0 - t * tm, logits, NEG_INF)
            m = jnp.maximum(jnp.max(logits, axis=-1, keepdims=True), sink)
            p = jnp.exp(logits - m)
            den = jnp.sum(p, axis=-1, keepdims=True) + jnp.exp(sink - m)
            oc = _dot(p.astype(BF16), vt[r0:r0 + nkeys]) * (1.0 / den)
            og = oc[:Lq]
            for gi in range(1, SWA_GROUP):
                og = jnp.where(lane_grp == gi, oc[gi * Lq:(gi + 1) * Lq], og)
            o_rows.append(og)
        o_cols.append(o_rows[0] if len(o_rows) == 1 else jnp.concatenate(o_rows, axis=0))
    o = jnp.concatenate(o_cols, axis=1)
    y_ref[...] = x + mod[2:3] * _dot(o.astype(BF16), wo_ref[...])
    if is_prompt:
        kc_ref[...] = k_new[tm - WINDOW:]
        vc_ref[...] = v_new[tm - WINDOW:]
    else:
        kc_ref[...] = k_new
        vc_ref[...] = v_new


def _swa_layer(x, mod, ng, w, cache, tm, Lq):
    B, T, D = x.shape
    is_prompt = cache is None
    wqkv, qn, kn, sinks, wo = w
    nk = SWA_KV_HEADS * SWA_HD
    in_specs = [
        pl.BlockSpec(memory_space=pltpu.SMEM),
        pl.BlockSpec((None, tm, D), lambda b, t: (b, t, 0)),
        pl.BlockSpec((None, ADA_CHUNKS, D), lambda b, t: (b, 0, 0)),
        _resident((1, D)),
        _resident(wqkv.shape), _resident(qn.shape), _resident(kn.shape), _resident(wo.shape),
    ]
    args = [sinks, x, mod, ng, wqkv, qn, kn, wo]
    if is_prompt:
        rows_out = WINDOW
    else:
        ck, cv = cache
        in_specs += [pl.BlockSpec((None, WINDOW, nk), lambda b, t: (b, 0, 0))] * 2
        args += [ck.reshape(B, WINDOW, nk), cv.reshape(B, WINDOW, nk)]
        rows_out = T
    cache_spec = pl.BlockSpec((None, rows_out, nk), lambda b, t: (b, 0, 0))
    y, kc, vc = pl.pallas_call(
        functools.partial(_swa_kernel, Lq=Lq, is_prompt=is_prompt),
        grid=(B, T // tm),
        in_specs=in_specs,
        out_specs=[pl.BlockSpec((None, tm, D), lambda b, t: (b, t, 0)), cache_spec, cache_spec],
        out_shape=[jax.ShapeDtypeStruct((B, T, D), F32),
                   jax.ShapeDtypeStruct((B, rows_out, nk), F32),
                   jax.ShapeDtypeStruct((B, rows_out, nk), F32)],
        scratch_shapes=[pltpu.VMEM((SWA_KV_HEADS, SWA_GROUP * Lq, WINDOW + Lq), F32)],
        compiler_params=_params(("arbitrary", "arbitrary")),
        name="swa_layer",
    )(*args)
    return (y, kc.reshape(B, rows_out, SWA_KV_HEADS, SWA_HD), vc.reshape(B, rows_out, SWA_KV_HEADS, SWA_HD))


def _tile(T, pref):
    return pref if T % pref == 0 else T


def kernel(x_prompt, x_sample, state_gla_l0, state_ret_l1, cache_swa_k_l2, cache_swa_v_l2, state_gla_l3, c_prompt, c_sample, norm_mix, norm_mlp, ada_w, ada_b, mlp_w1, mlp_w2, gla_wq, gla_wk, gla_wv, gla_wg1, gla_wg2, gla_bg, gla_wr, gla_norm, gla_wo, ret_wq, ret_wk, ret_wv, ret_wg, ret_wo, swa_wqkv, swa_qnorm, swa_knorm, swa_sinks, swa_wo):
    bp = x_prompt.shape[0]
    bs = x_sample.shape[0]
    Tp = x_prompt.shape[1]
    Ts = x_sample.shape[1]
    gla_states = [state_gla_l0, state_gla_l3]
    ret_states = [state_ret_l1]
    swa_caches = [(cache_swa_k_l2, cache_swa_v_l2)]

    mod_all = _ada_modulation(jnp.concatenate([c_prompt, c_sample], axis=0), ada_w, ada_b)
    mod_all = mod_all.reshape(DEPTH, bp + bs, ADA_CHUNKS, D_MODEL)

    yp, ys = x_prompt, x_sample
    new_state = []
    for layer in range(DEPTH):
        kind, j = layer % N_MIXERS, layer // N_MIXERS
        mod_p = mod_all[layer, :bp]
        mod_s = mod_all[layer, bp:]
        ng = norm_mix[layer].reshape(1, D_MODEL)
        if kind == 0:
            w = (gla_wq[j].astype(BF16), gla_wk[j].astype(BF16), gla_wv[j].astype(BF16), gla_wg1[j].astype(BF16),
                 gla_wg2[j].astype(BF16), gla_bg[j].reshape(1, -1), gla_wr[j].astype(BF16),
                 gla_norm[j].reshape(1, -1), gla_wo[j].astype(BF16))
            yp, st_p = _gla_layer(yp, mod_p, ng, w, None, _tile(Tp, 256), min(CHUNK, Tp))
            ys, st_s = _gla_layer(ys, mod_s, ng, w, gla_states[j], Ts, min(CHUNK, Ts))
            new_state += [st_p, st_s]
        elif kind == 1:
            w = (ret_wq[j].astype(BF16), ret_wk[j].astype(BF16), ret_wv[j].astype(BF16), ret_wg[j].astype(BF16),
                 ret_wo[j].astype(BF16))
            yp, st_p = _ret_layer(yp, mod_p, ng, w, None, _tile(Tp, 256))
            ys, st_s = _ret_layer(ys, mod_s, ng, w, ret_states[j], Ts)
            new_state += [st_p, st_s]
        else:
            w = (swa_wqkv[j].astype(BF16), jnp.tile(swa_qnorm[j], SWA_Q_HEADS).reshape(1, -1),
                 jnp.tile(swa_knorm[j], SWA_KV_HEADS).reshape(1, -1), swa_sinks[j], swa_wo[j].astype(BF16))
            yp, k_p, v_p = _swa_layer(yp, mod_p, ng, w, None, _tile(Tp, 256), CHUNK)
            ys, k_s, v_s = _swa_layer(ys, mod_s, ng, w, swa_caches[j], Ts, Ts)
            new_state += [k_p, v_p, k_s, v_s]
        gm = norm_mlp[layer].reshape(1, D_MODEL)
        w1 = mlp_w1[layer].astype(BF16)
        w2 = mlp_w2[layer].astype(BF16)
        yp = _mlp_layer(yp, mod_p, gm, w1, w2, _tile(Tp, 512))
        ys = _mlp_layer(ys, mod_s, gm, w1, w2, Ts)
    return (yp, ys, *new_state)
```

```python
import functools
import math

import jax
import jax.numpy as jnp
from jax import lax
from jax.experimental import pallas as pl
from jax.experimental.pallas import tpu as pltpu

F32 = jnp.float32
BF16 = jnp.bfloat16

D_MODEL = 1024
DEPTH = 4
CHUNK = 64
N_MIXERS = 3
GLA_HEADS = 4
GLA_DK = 128
GLA_DV = 256
GLA_RANK = 16
GLA_TAU = 16.0
GLA_RB = 256
RET_HEADS = 4
RET_DK = 256
RET_DV = 512
SWA_Q_HEADS = 16
SWA_KV_HEADS = 4
SWA_GROUP = 4
SWA_HD = 64
WINDOW = 128
MLP_HIDDEN = 4 * D_MODEL
ADA_CHUNKS = 6
NORM_EPS = 1e-6
GN_EPS = 1e-5
NEG_INF = -1e30
LOG2E = 1.4426950408889634

VMEM_LIMIT_BYTES = 56 * 1024 * 1024
LANES = 128


def _dot(a, b):
    return jnp.dot(a, b, preferred_element_type=F32)


def _dot_nt(a, b):
    return lax.dot_general(a, b, (((1,), (1,)), ((), ())), preferred_element_type=F32)


def _pad_rows(a, mult):
    r = (-a.shape[0]) % mult
    return a if r == 0 else jnp.concatenate([a, jnp.zeros((r, a.shape[1]), a.dtype)], axis=0)


def _modulated_norm(x, g, shift, scale):
    ms = jnp.mean(x * x, axis=-1, keepdims=True)
    return x * lax.rsqrt(ms + NORM_EPS) * (g * (1.0 + scale)) + shift


def _silu(x):
    return x * (1.0 / (1.0 + jnp.exp2(x * -LOG2E)))


def _resident(shape):
    nd = len(shape)
    return pl.BlockSpec(shape, lambda *_: (0,) * nd, pipeline_mode=pl.Buffered(1))


def _params(sem):
    return pltpu.CompilerParams(dimension_semantics=sem, vmem_limit_bytes=VMEM_LIMIT_BYTES)


def _ada_kernel(c_ref, w_ref, b_ref, o_ref):
    c = c_ref[...]
    a = _silu(c).astype(BF16)
    o_ref[...] = _dot(a, w_ref[...].astype(BF16)) + b_ref[...]


def _ada_modulation(c_all, ada_w, ada_b):
    nb = c_all.shape[0]
    width = ADA_CHUNKS * D_MODEL
    tn = 1536
    return pl.pallas_call(
        _ada_kernel,
        grid=(DEPTH, width // tn),
        in_specs=[
            pl.BlockSpec((nb, D_MODEL), lambda l, j: (0, 0)),
            pl.BlockSpec((None, D_MODEL, tn), lambda l, j: (l, 0, j)),
            pl.BlockSpec((None, 1, tn), lambda l, j: (l, 0, j)),
        ],
        out_specs=pl.BlockSpec((None, nb, tn), lambda l, j: (l, 0, j)),
        out_shape=jax.ShapeDtypeStruct((DEPTH, nb, width), F32),
        compiler_params=_params(("arbitrary", "arbitrary")),
        name="ada_modulation",
    )(c_all, ada_w, ada_b.reshape(DEPTH, 1, width))


def _mlp_kernel(x_ref, mod_ref, g_ref, w1_ref, w2_ref, o_ref, *, th):
    x = x_ref[...]
    h = _modulated_norm(x, g_ref[...], mod_ref[3], mod_ref[4]).astype(BF16)
    acc = jnp.zeros(x.shape, F32)
    for c in range(MLP_HIDDEN // th):
        a = _dot(h, w1_ref[:, c * th:(c + 1) * th])
        a = jnp.maximum(a, 0.0)
        a = a * a
        acc = acc + _dot(a.astype(BF16), w2_ref[c * th:(c + 1) * th, :])
    o_ref[...] = x + mod_ref[5] * acc


def _mlp_layer(x, mod, g, w1, w2, tm):
    B, T, D = x.shape
    return pl.pallas_call(
        functools.partial(_mlp_kernel, th=512),
        grid=(B, T // tm),
        in_specs=[
            pl.BlockSpec((None, tm, D), lambda b, t: (b, t, 0)),
            pl.BlockSpec((None, ADA_CHUNKS, mod.shape[2], D), lambda b, t: (b, 0, 0, 0)),
            _resident((1, D)),
            _resident((D, MLP_HIDDEN)),
            _resident((MLP_HIDDEN, D)),
        ],
        out_specs=pl.BlockSpec((None, tm, D), lambda b, t: (b, t, 0)),
        out_shape=jax.ShapeDtypeStruct((B, T, D), F32),
        compiler_params=_params(("arbitrary", "arbitrary")),
        name="mlp_layer",
    )(x, mod, g, w1, w2)


def _chunk_cumsum(a, L):
    row = lax.broadcasted_iota(jnp.int32, a.shape, 0) % L
    s = 1
    while s < L:
        a = a + jnp.where(row >= s, pltpu.roll(a, s, axis=0), 0.0)
        s *= 2
    return a


def _gla_kernel(*refs, L, has_state):
    if has_state:
        (x_ref, mod_ref, ng_ref, wqk_ref, wv_ref, wg1_ref, wg2_ref, bg_ref, wr_ref, gn_ref, wo_ref,
         s0_ref, y_ref, s_ref) = refs
    else:
        (x_ref, mod_ref, ng_ref, wqk_ref, wv_ref, wg1_ref, wg2_ref, bg_ref, wr_ref, gn_ref, wo_ref,
         y_ref, s_ref) = refs

    @pl.when(pl.program_id(1) == 0)
    def _():
        if has_state:
            s_ref[...] = s0_ref[...]
        else:
            s_ref[...] = jnp.zeros(s_ref.shape, F32)

    x = x_ref[...]
    tm = x.shape[0]
    nc = tm // L
    mod = mod_ref[...]
    hb = _modulated_norm(x, ng_ref[...], mod[0:1], mod[1:2]).astype(BF16)
    g1 = _dot(hb, wg1_ref[...]).astype(BF16)
    rb = min(tm, GLA_RB)
    ncb = rb // L
    ri = lax.broadcasted_iota(jnp.int32, (rb, rb), 0)
    ci = lax.broadcasted_iota(jnp.int32, (rb, rb), 1)
    mask = ri >= ci
    if ncb > 1:
        mask = mask & (ci >= (ri // L) * L)
    rbp = rb + (-rb) % LANES
    lane_chunk = lax.broadcasted_iota(jnp.int32, (GLA_DK, rbp), 1) // L
    gn = gn_ref[...]

    o_heads = []
    for h in range(GLA_HEADS):
        kc = slice(h * GLA_DK, (h + 1) * GLA_DK)
        vc = slice(h * GLA_DV, (h + 1) * GLA_DV)
        z = _dot(g1, wg2_ref[:, kc]) + bg_ref[:, kc]
        logg = (jnp.minimum(z, 0.0) - jnp.log(1.0 + jnp.exp(-jnp.abs(z)))) * (1.0 / GLA_TAU)
        b = _chunk_cumsum(logg, L)
        eb = jnp.exp(b)
        enb = jnp.exp(-b)
        ebl = [jnp.exp(b[c * L + L - 1:c * L + L, :]) for c in range(nc)]
        ebl_rows = [jnp.broadcast_to(e, (L, GLA_DK)) for e in ebl]
        ebl_rows = ebl_rows[0] if nc == 1 else jnp.concatenate(ebl_rows, axis=0)
        qk = _dot(hb, wqk_ref[:, 2 * h * GLA_DK:2 * (h + 1) * GLA_DK])
        qd = (qk[:, :GLA_DK] * (eb * GLA_DK ** -0.5)).astype(BF16)
        kdf = qk[:, GLA_DK:] * enb
        kd = kdf.astype(BF16)
        kdec = kdf * ebl_rows
        v = _dot(hb, wv_ref[:, vc]).astype(BF16)
        gate = _silu(_dot(hb, wr_ref[:, vc]))
        S = s_ref[h]
        o_blocks = []
        for r0 in range(0, tm, rb):
            rows = slice(r0, r0 + rb)
            vh = _pad_rows(v[rows], LANES)
            att = jnp.where(mask, _dot_nt(qd[rows], kd[rows]), 0.0).astype(BF16)
            o = _dot(att, vh[:rb])
            kdec_t = _pad_rows(kdec[rows], LANES).T
            lhs = [jnp.where(lane_chunk == c, kdec_t, 0.0) for c in range(ncb)]
            lhs = (lhs[0] if ncb == 1 else jnp.concatenate(lhs, axis=0)).astype(BF16)
            u = _dot(lhs, vh)
            o_inter = []
            for c in range(ncb):
                o_inter.append(_dot(qd[r0 + c * L:r0 + (c + 1) * L], S.astype(BF16)))
                dcol = jnp.broadcast_to(ebl[r0 // L + c], (GLA_DK, GLA_DK)).T
                S = S * jnp.concatenate([dcol, dcol], axis=1) + u[c * GLA_DK:(c + 1) * GLA_DK]
            o_blocks.append(o + (o_inter[0] if ncb == 1 else jnp.concatenate(o_inter, axis=0)))
        s_ref[h] = S
        o = o_blocks[0] if len(o_blocks) == 1 else jnp.concatenate(o_blocks, axis=0)
        o = o * lax.rsqrt(jnp.mean(o * o, axis=-1, keepdims=True) + NORM_EPS) * gn
        o_heads.append((o * gate).astype(BF16))
    y_ref[...] = x + mod[2:3] * _dot(jnp.concatenate(o_heads, axis=1), wo_ref[...])


def _gla_layer(x, mod, ng, w, s0, tm, L):
    B, T, D = x.shape
    has_state = s0 is not None
    wqk, wv, wg1, wg2, bg, wr, gn, wo = w
    state_spec = pl.BlockSpec((None, GLA_HEADS, GLA_DK, GLA_DV), lambda b, t: (b, 0, 0, 0))
    in_specs = [
        pl.BlockSpec((None, tm, D), lambda b, t: (b, t, 0)),
        pl.BlockSpec((None, ADA_CHUNKS, D), lambda b, t: (b, 0, 0)),
        _resident((1, D)),
        _resident(wqk.shape), _resident(wv.shape), _resident(wg1.shape),
        _resident(wg2.shape), _resident(bg.shape), _resident(wr.shape), _resident(gn.shape), _resident(wo.shape),
    ]
    args = [x, mod, ng, wqk, wv, wg1, wg2, bg, wr, gn, wo]
    if has_state:
        in_specs.append(state_spec)
        args.append(s0)
    return pl.pallas_call(
        functools.partial(_gla_kernel, L=L, has_state=has_state),
        grid=(B, T // tm),
        in_specs=in_specs,
        out_specs=[pl.BlockSpec((None, tm, D), lambda b, t: (b, t, 0)), state_spec],
        out_shape=[jax.ShapeDtypeStruct((B, T, D), F32),
                   jax.ShapeDtypeStruct((B, GLA_HEADS, GLA_DK, GLA_DV), F32)],
        compiler_params=_params(("arbitrary", "arbitrary")),
        name="gla_layer",
    )(*args)


def _ret_kernel(*refs, has_state):
    if has_state:
        (x_ref, mod_ref, ng_ref, wq_ref, wk_ref, wv_ref, wg_ref, wo_ref, s0_ref, y_ref, s_ref,
         dintra_ref, dquery_ref, dkey_ref) = refs
    else:
        (x_ref, mod_ref, ng_ref, wq_ref, wk_ref, wv_ref, wg_ref, wo_ref, y_ref, s_ref,
         dintra_ref, dquery_ref, dkey_ref) = refs
    L = x_ref.shape[0]
    log_decay = [math.log(1.0 - 2.0 ** (-5.0 - h)) for h in range(RET_HEADS)]

    @pl.when((pl.program_id(0) == 0) & (pl.program_id(1) == 0))
    def _():
        rel = (lax.broadcasted_iota(jnp.int32, (L, L), 0) - lax.broadcasted_iota(jnp.int32, (L, L), 1)).astype(F32)
        pos = lax.broadcasted_iota(jnp.int32, (L, LANES), 0).astype(F32)
        for h in range(RET_HEADS):
            dintra_ref[h] = jnp.where(rel >= 0.0, jnp.exp(log_decay[h] * jnp.maximum(rel, 0.0)), 0.0)
            dquery_ref[h] = jnp.exp(log_decay[h] * (pos + 1.0))
            dkey_ref[h] = jnp.exp(log_decay[h] * (L - 1.0 - pos))

    @pl.when(pl.program_id(1) == 0)
    def _():
        if has_state:
            s_ref[...] = s0_ref[...]
        else:
            s_ref[...] = jnp.zeros(s_ref.shape, F32)

    x = x_ref[...]
    mod = mod_ref[...]
    hb = _modulated_norm(x, ng_ref[...], mod[0:1], mod[1:2]).astype(BF16)
    q = _dot(hb, wq_ref[...])
    k = _dot(hb, wk_ref[...]) * (RET_DK ** -0.5)
    v = _dot(hb, wv_ref[...])
    g = _dot(hb, wg_ref[...])
    qb = q.astype(BF16)
    kb = k.astype(BF16)
    vb = _pad_rows(v.astype(BF16), LANES)

    o_heads = []
    for h in range(RET_HEADS):
        kc = slice(h * RET_DK, (h + 1) * RET_DK)
        vh = vb[:, h * RET_DV:(h + 1) * RET_DV]
        dq = dquery_ref[h]
        dk = dkey_ref[h]
        att = (_dot_nt(qb[:, kc], kb[:, kc]) * dintra_ref[h]).astype(BF16)
        S = s_ref[h]
        o = _dot(att, vh[:L]) + _dot(qb[:, kc], S.astype(BF16)) * jnp.concatenate([dq] * (RET_DV // LANES), axis=1)
        k_dec = _pad_rows(k[:, kc] * jnp.concatenate([dk] * (RET_DK // LANES), axis=1), LANES)
        s_ref[h] = S * math.exp(log_decay[h] * L) + _dot(k_dec.T.astype(BF16), vh)
        oc = o - jnp.mean(o, axis=-1, keepdims=True)
        o_heads.append(oc * lax.rsqrt(jnp.mean(oc * oc, axis=-1, keepdims=True) + GN_EPS))
    o = jnp.concatenate(o_heads, axis=1)
    o = (o * _silu(g)).astype(BF16)
    y_ref[...] = x + mod[2:3] * _dot(o, wo_ref[...])


def _ret_layer(x, mod, ng, w, s0, tm):
    B, T, D = x.shape
    has_state = s0 is not None
    wq, wk, wv, wg, wo = w
    state_spec = pl.BlockSpec((None, RET_HEADS, RET_DK, RET_DV), lambda b, t: (b, 0, 0, 0))
    in_specs = [
        pl.BlockSpec((None, tm, D), lambda b, t: (b, t, 0)),
        pl.BlockSpec((None, ADA_CHUNKS, D), lambda b, t: (b, 0, 0)),
        _resident((1, D)),
        _resident(wq.shape), _resident(wk.shape), _resident(wv.shape), _resident(wg.shape), _resident(wo.shape),
    ]
    args = [x, mod, ng, wq, wk, wv, wg, wo]
    if has_state:
        in_specs.append(state_spec)
        args.append(s0)
    return pl.pallas_call(
        functools.partial(_ret_kernel, has_state=has_state),
        grid=(B, T // tm),
        in_specs=in_specs,
        out_specs=[pl.BlockSpec((None, tm, D), lambda b, t: (b, t, 0)), state_spec],
        out_shape=[jax.ShapeDtypeStruct((B, T, D), F32),
                   jax.ShapeDtypeStruct((B, RET_HEADS, RET_DK, RET_DV), F32)],
        scratch_shapes=[pltpu.VMEM((RET_HEADS, tm, tm), F32), pltpu.VMEM((RET_HEADS, tm, LANES), F32),
                        pltpu.VMEM((RET_HEADS, tm, LANES), F32)],
        compiler_params=_params(("arbitrary", "arbitrary")),
        name="ret_layer",
    )(*args)


def _head_mean_square(a, hsum):
    w = hsum.shape[0]
    parts = [_dot((a[:, j:j + w] * a[:, j:j + w]).astype(BF16), hsum) for j in range(0, a.shape[1], w)]
    return parts[0] if len(parts) == 1 else jnp.concatenate(parts, axis=1)


def _tile_head_lanes(a, kv):
    t = a[:, (kv // 2) * LANES:(kv // 2 + 1) * LANES]
    low = lax.broadcasted_iota(jnp.int32, t.shape, 1) < SWA_HD
    keep = low if kv % 2 == 0 else jnp.logical_not(low)
    t = jnp.where(keep, t, pltpu.roll(t, SWA_HD, axis=1)).astype(BF16)
    return jnp.concatenate([t, t], axis=1)


def _swa_kernel(*refs, Lq, is_prompt):
    if is_prompt:
        (sink_ref, x_ref, mod_ref, ng_ref, wqkv_ref, qn_ref, kn_ref, wo_ref, y_ref, kc_ref, vc_ref,
         bias_ref) = refs
    else:
        (sink_ref, x_ref, mod_ref, ng_ref, wqkv_ref, qn_ref, kn_ref, wo_ref, ck_ref, cv_ref,
         y_ref, kc_ref, vc_ref, bias_ref) = refs
    t = pl.program_id(1)
    x = x_ref[...]
    tm = x.shape[0]
    nq = SWA_Q_HEADS * SWA_HD
    nk = SWA_KV_HEADS * SWA_HD
    gw = SWA_GROUP * SWA_HD
    nkeys = WINDOW + Lq
    rows_q = SWA_GROUP * Lq

    @pl.when((pl.program_id(0) == 0) & (t == 0))
    def _():
        ri = lax.broadcasted_iota(jnp.int32, (rows_q, nkeys), 0)
        ji = lax.broadcasted_iota(jnp.int32, (rows_q, nkeys), 1)
        grp = ri // Lq
        dist = jnp.abs((ri - grp * Lq) - ji + WINDOW).astype(F32)
        for kv in range(SWA_KV_HEADS):
            bias_ref[kv] = -LOG2E * jnp.exp2(-0.5 * (kv * SWA_GROUP + grp + 1).astype(F32)) * dist

    mod = mod_ref[...]
    hb = _modulated_norm(x, ng_ref[...], mod[0:1], mod[1:2]).astype(BF16)
    qkv = _dot(hb, wqkv_ref[...])
    hi = lax.broadcasted_iota(jnp.int32, (gw, gw), 0) // SWA_HD
    hj = lax.broadcasted_iota(jnp.int32, (gw, gw), 1) // SWA_HD
    hsum = jnp.where(hi == hj, 1.0 / SWA_HD, 0.0).astype(BF16)
    q = qkv[:, :nq]
    k = qkv[:, nq:nq + nk]
    q = (q * lax.rsqrt(_head_mean_square(q, hsum) + NORM_EPS) * (qn_ref[...] * (SWA_HD ** -0.5 * LOG2E))).astype(BF16)
    k_new = k * lax.rsqrt(_head_mean_square(k, hsum) + NORM_EPS) * kn_ref[...]
    v_new = qkv[:, nq + nk:]

    if is_prompt:
        @pl.when(t == 0)
        def _():
            kc_ref[...] = jnp.zeros(kc_ref.shape, F32)
            vc_ref[...] = jnp.zeros(vc_ref.shape, F32)
        k_prev = kc_ref[...]
        v_prev = vc_ref[...]
    else:
        k_prev = ck_ref[...]
        v_prev = cv_ref[...]
    k_ext = jnp.concatenate([k_prev, k_new], axis=0)
    v_ext = jnp.concatenate([v_prev, v_new], axis=0)

    lane_grp = lax.broadcasted_iota(jnp.int32, (Lq, gw), 1) // SWA_HD
    grp_col = lax.broadcasted_iota(jnp.int32, (rows_q, 1), 0) // Lq
    ji = lax.broadcasted_iota(jnp.int32, (rows_q, nkeys), 1)
    zero = jnp.zeros((), BF16)

    o_cols = []
    for kv in range(SWA_KV_HEADS):
        kt = _tile_head_lanes(k_ext, kv)
        vt = _tile_head_lanes(v_ext, kv)
        bias = bias_ref[kv]
        sink = jnp.zeros((rows_q, 1), F32)
        for gi in range(SWA_GROUP):
            sink = jnp.where(grp_col == gi, sink_ref[kv * SWA_GROUP + gi] * LOG2E, sink)
        o_rows = []
        for c in range(tm // Lq):
            r0 = c * Lq
            qc = q[r0:r0 + Lq, kv * gw:(kv + 1) * gw]
            qs = jnp.concatenate([jnp.where(lane_grp == gi, qc, zero) for gi in range(SWA_GROUP)], axis=0)
            logits = _dot_nt(qs, kt[r0:r0 + nkeys]) + bias
            if is_prompt and r0 < WINDOW:
                logits = jnp.where(ji >= WINDOW - r0 - t * tm, logits, NEG_INF)
            m = jnp.maximum(jnp.max(logits, axis=-1, keepdims=True), sink)
            p = jnp.exp2(logits - m)
            den = jnp.sum(p, axis=-1, keepdims=True) + jnp.exp2(sink - m)
            oc = _dot(p.astype(BF16), vt[r0:r0 + nkeys]) * (1.0 / den)
            og = oc[:Lq]
            for gi in range(1, SWA_GROUP):
                og = jnp.where(lane_grp == gi, oc[gi * Lq:(gi + 1) * Lq], og)
            o_rows.append(og)
        o_cols.append(o_rows[0] if len(o_rows) == 1 else jnp.concatenate(o_rows, axis=0))
    o = jnp.concatenate(o_cols, axis=1)
    y_ref[...] = x + mod[2:3] * _dot(o.astype(BF16), wo_ref[...])
    if is_prompt:
        kc_ref[...] = k_new[tm - WINDOW:]
        vc_ref[...] = v_new[tm - WINDOW:]
    else:
        kc_ref[...] = k_new
        vc_ref[...] = v_new


def _swa_layer(x, mod, ng, w, cache, tm, Lq):
    B, T, D = x.shape
    is_prompt = cache is None
    wqkv, qn, kn, sinks, wo = w
    nk = SWA_KV_HEADS * SWA_HD
    in_specs = [
        pl.BlockSpec(memory_space=pltpu.SMEM),
        pl.BlockSpec((None, tm, D), lambda b, t: (b, t, 0)),
        pl.BlockSpec((None, ADA_CHUNKS, D), lambda b, t: (b, 0, 0)),
        _resident((1, D)),
        _resident(wqkv.shape), _resident(qn.shape), _resident(kn.shape), _resident(wo.shape),
    ]
    args = [sinks, x, mod, ng, wqkv, qn, kn, wo]
    if is_prompt:
        rows_out = WINDOW
    else:
        ck, cv = cache
        in_specs += [pl.BlockSpec((None, WINDOW, nk), lambda b, t: (b, 0, 0))] * 2
        args += [ck.reshape(B, WINDOW, nk), cv.reshape(B, WINDOW, nk)]
        rows_out = T
    cache_spec = pl.BlockSpec((None, rows_out, nk), lambda b, t: (b, 0, 0))
    y, kc, vc = pl.pallas_call(
        functools.partial(_swa_kernel, Lq=Lq, is_prompt=is_prompt),
        grid=(B, T // tm),
        in_specs=in_specs,
        out_specs=[pl.BlockSpec((None, tm, D), lambda b, t: (b, t, 0)), cache_spec, cache_spec],
        out_shape=[jax.ShapeDtypeStruct((B, T, D), F32),
                   jax.ShapeDtypeStruct((B, rows_out, nk), F32),
                   jax.ShapeDtypeStruct((B, rows_out, nk), F32)],
        scratch_shapes=[pltpu.VMEM((SWA_KV_HEADS, SWA_GROUP * Lq, WINDOW + Lq), F32)],
        compiler_params=_params(("arbitrary", "arbitrary")),
        name="swa_layer",
    )(*args)
    return (y, kc.reshape(B, rows_out, SWA_KV_HEADS, SWA_HD), vc.reshape(B, rows_out, SWA_KV_HEADS, SWA_HD))


def _tile(T, pref):
    return pref if T % pref == 0 else T


def kernel(x_prompt, x_sample, state_gla_l0, state_ret_l1, cache_swa_k_l2, cache_swa_v_l2, state_gla_l3, c_prompt, c_sample, norm_mix, norm_mlp, ada_w, ada_b, mlp_w1, mlp_w2, gla_wq, gla_wk, gla_wv, gla_wg1, gla_wg2, gla_bg, gla_wr, gla_norm, gla_wo, ret_wq, ret_wk, ret_wv, ret_wg, ret_wo, swa_wqkv, swa_qnorm, swa_knorm, swa_sinks, swa_wo):
    bp = x_prompt.shape[0]
    bs = x_sample.shape[0]
    Tp = x_prompt.shape[1]
    Ts = x_sample.shape[1]
    gla_states = [state_gla_l0, state_gla_l3]
    ret_states = [state_ret_l1]
    swa_caches = [(cache_swa_k_l2, cache_swa_v_l2)]

    mod_all = _ada_modulation(jnp.concatenate([c_prompt, c_sample], axis=0), ada_w, ada_b)
    mod_all = mod_all.reshape(DEPTH, bp + bs, ADA_CHUNKS, D_MODEL)

    yp, ys = x_prompt, x_sample
    new_state = []
    for layer in range(DEPTH):
        kind, j = layer % N_MIXERS, layer // N_MIXERS
        mod_p = mod_all[layer, :bp]
        mod_s = mod_all[layer, bp:]
        ng = norm_mix[layer].reshape(1, D_MODEL)
        if kind == 0:
            wqk = jnp.concatenate([gla_wq[j].reshape(D_MODEL, GLA_HEADS, GLA_DK),
                                   gla_wk[j].reshape(D_MODEL, GLA_HEADS, GLA_DK)], axis=2).reshape(D_MODEL, -1)
            w = (wqk.astype(BF16), gla_wv[j].astype(BF16), gla_wg1[j].astype(BF16),
                 gla_wg2[j].astype(BF16), gla_bg[j].reshape(1, -1), gla_wr[j].astype(BF16),
                 gla_norm[j].reshape(1, -1), gla_wo[j].astype(BF16))
            yp, st_p = _gla_layer(yp, mod_p, ng, w, None, _tile(Tp, 512), min(CHUNK, Tp))
            ys, st_s = _gla_layer(ys, mod_s, ng, w, gla_states[j], Ts, min(CHUNK, Ts))
            new_state += [st_p, st_s]
        elif kind == 1:
            w = (ret_wq[j].astype(BF16), ret_wk[j].astype(BF16), ret_wv[j].astype(BF16), ret_wg[j].astype(BF16),
                 ret_wo[j].astype(BF16))
            yp, st_p = _ret_layer(yp, mod_p, ng, w, None, _tile(Tp, 256))
            ys, st_s = _ret_layer(ys, mod_s, ng, w, ret_states[j], Ts)
            new_state += [st_p, st_s]
        else:
            w = (swa_wqkv[j].astype(BF16), jnp.tile(swa_qnorm[j], SWA_Q_HEADS).reshape(1, -1),
                 jnp.tile(swa_knorm[j], SWA_KV_HEADS).reshape(1, -1), swa_sinks[j], swa_wo[j].astype(BF16))
            yp, k_p, v_p = _swa_layer(yp, mod_p, ng, w, None, _tile(Tp, 512), CHUNK)
            ys, k_s, v_s = _swa_layer(ys, mod_s, ng, w, swa_caches[j], Ts, Ts)
            new_state += [k_p, v_p, k_s, v_s]
        gm = norm_mlp[layer].reshape(1, D_MODEL)
        w1 = mlp_w1[layer].astype(BF16)
        w2 = mlp_w2[layer].astype(BF16)
        yp = _mlp_layer(yp, mod_p[:, :, None, :], gm, w1, w2, _tile(Tp, 512))
        mod_rows = jnp.repeat(mod_s.transpose(1, 0, 2), Ts, axis=1)[None]
        ys = _mlp_layer(ys.reshape(1, bs * Ts, D_MODEL), mod_rows, gm, w1, w2, bs * Ts).reshape(bs, Ts, D_MODEL)
    return (yp, ys, *new_state)
```

```python
import functools
import math

import jax
import jax.numpy as jnp
from jax import lax
from jax.experimental import pallas as pl
from jax.experimental.pallas import tpu as pltpu

F32 = jnp.float32
BF16 = jnp.bfloat16

D_MODEL = 1024
DEPTH = 4
CHUNK = 64
N_MIXERS = 3
GLA_HEADS = 4
GLA_DK = 128
GLA_DV = 256
GLA_RANK = 16
GLA_TAU = 16.0
GLA_RB = 256
GLA_LOCKSTEP = 2
RET_HEADS = 4
RET_DK = 256
RET_DV = 512
RET_RB = 256
RET_LOCKSTEP = 2
SWA_Q_HEADS = 16
SWA_KV_HEADS = 4
SWA_GROUP = 4
SWA_HD = 64
WINDOW = 128
SWA_LOCKSTEP = 4
MLP_HIDDEN = 4 * D_MODEL
ADA_CHUNKS = 6
NORM_EPS = 1e-6
GN_EPS = 1e-5
NEG_INF = -1e30
LOG2E = 1.4426950408889634

VMEM_LIMIT_BYTES = 56 * 1024 * 1024
LANES = 128


def _dot(a, b):
    return jnp.dot(a, b, preferred_element_type=F32)


def _dot_nt(a, b):
    return lax.dot_general(a, b, (((1,), (1,)), ((), ())), preferred_element_type=F32)


def _pad_rows(a, mult):
    r = (-a.shape[0]) % mult
    return a if r == 0 else jnp.concatenate([a, jnp.zeros((r, a.shape[1]), a.dtype)], axis=0)


def _modulated_norm(x, g, shift, scale):
    ms = jnp.mean(x * x, axis=-1, keepdims=True)
    return x * lax.rsqrt(ms + NORM_EPS) * (g * (1.0 + scale)) + shift


def _silu(x):
    return x * (1.0 / (1.0 + jnp.exp2(x * -LOG2E)))


def _resident(shape):
    nd = len(shape)
    return pl.BlockSpec(shape, lambda *_: (0,) * nd, pipeline_mode=pl.Buffered(1))


def _params(sem):
    return pltpu.CompilerParams(dimension_semantics=sem, vmem_limit_bytes=VMEM_LIMIT_BYTES)


def _ada_kernel(c_ref, w_ref, b_ref, o_ref):
    c = c_ref[...]
    a = _silu(c).astype(BF16)
    o_ref[...] = _dot(a, w_ref[...].astype(BF16)) + b_ref[...]


def _ada_modulation(c_all, ada_w, ada_b):
    nb = c_all.shape[0]
    width = ADA_CHUNKS * D_MODEL
    tn = 1536
    return pl.pallas_call(
        _ada_kernel,
        grid=(DEPTH, width // tn),
        in_specs=[
            pl.BlockSpec((nb, D_MODEL), lambda l, j: (0, 0)),
            pl.BlockSpec((None, D_MODEL, tn), lambda l, j: (l, 0, j)),
            pl.BlockSpec((None, 1, tn), lambda l, j: (l, 0, j)),
        ],
        out_specs=pl.BlockSpec((None, nb, tn), lambda l, j: (l, 0, j)),
        out_shape=jax.ShapeDtypeStruct((DEPTH, nb, width), F32),
        compiler_params=_params(("arbitrary", "arbitrary")),
        name="ada_modulation",
    )(c_all, ada_w, ada_b.reshape(DEPTH, 1, width))


def _mlp_kernel(x_ref, mod_ref, g_ref, w1_ref, w2_ref, o_ref, *, th):
    x = x_ref[...]
    h = _modulated_norm(x, g_ref[...], mod_ref[3], mod_ref[4]).astype(BF16)
    acc = jnp.zeros(x.shape, F32)
    for c in range(MLP_HIDDEN // th):
        a = _dot(h, w1_ref[:, c * th:(c + 1) * th])
        a = jnp.maximum(a, 0.0)
        a = a * a
        acc = acc + _dot(a.astype(BF16), w2_ref[c * th:(c + 1) * th, :])
    o_ref[...] = x + mod_ref[5] * acc


def _mlp_layer(x, mod, g, w1, w2, tm):
    B, T, D = x.shape
    return pl.pallas_call(
        functools.partial(_mlp_kernel, th=512),
        grid=(B, T // tm),
        in_specs=[
            pl.BlockSpec((None, tm, D), lambda b, t: (b, t, 0)),
            pl.BlockSpec((None, ADA_CHUNKS, mod.shape[2], D), lambda b, t: (b, 0, 0, 0)),
            _resident((1, D)),
            _resident((D, MLP_HIDDEN)),
            _resident((MLP_HIDDEN, D)),
        ],
        out_specs=pl.BlockSpec((None, tm, D), lambda b, t: (b, t, 0)),
        out_shape=jax.ShapeDtypeStruct((B, T, D), F32),
        compiler_params=_params(("arbitrary", "arbitrary")),
        name="mlp_layer",
    )(x, mod, g, w1, w2)


def _chunk_cumsum(a, L):
    row = lax.broadcasted_iota(jnp.int32, a.shape, 0) % L
    s = 1
    while s < L:
        a = a + jnp.where(row >= s, pltpu.roll(a, s, axis=0), 0.0)
        s *= 2
    return a


def _gla_kernel(*refs, L, has_state):
    if has_state:
        (x_ref, mod_ref, ng_ref, wqk_ref, wv_ref, wg1_ref, wg2_ref, bg_ref, wr_ref, gn_ref, wo_ref,
         s0_ref, y_ref, s_ref) = refs
    else:
        (x_ref, mod_ref, ng_ref, wqk_ref, wv_ref, wg1_ref, wg2_ref, bg_ref, wr_ref, gn_ref, wo_ref,
         y_ref, s_ref) = refs

    @pl.when(pl.program_id(1) == 0)
    def _():
        if has_state:
            s_ref[...] = s0_ref[...]
        else:
            s_ref[...] = jnp.zeros(s_ref.shape, F32)

    x = x_ref[...]
    tm = x.shape[0]
    nc = tm // L
    mod = mod_ref[...]
    hb = _modulated_norm(x, ng_ref[...], mod[0:1], mod[1:2]).astype(BF16)
    g1 = _dot(hb, wg1_ref[...]).astype(BF16)
    rb = min(tm, GLA_RB)
    ncb = rb // L
    ri = lax.broadcasted_iota(jnp.int32, (rb, rb), 0)
    ci = lax.broadcasted_iota(jnp.int32, (rb, rb), 1)
    mask = ri >= ci
    if ncb > 1:
        mask = mask & (ci >= (ri // L) * L)
    rbp = rb + (-rb) % LANES
    lane_chunk = lax.broadcasted_iota(jnp.int32, (GLA_DK, rbp), 1) // L
    gn = gn_ref[...]

    o_heads = []
    for h0 in range(0, GLA_HEADS, GLA_LOCKSTEP):
        hs = list(range(h0, min(h0 + GLA_LOCKSTEP, GLA_HEADS)))
        kcs = [slice(h * GLA_DK, (h + 1) * GLA_DK) for h in hs]
        vcs = [slice(h * GLA_DV, (h + 1) * GLA_DV) for h in hs]
        zs = [_dot(g1, wg2_ref[:, kc]) + bg_ref[:, kc] for kc in kcs]
        loggs = [(jnp.minimum(z, 0.0) - jnp.log(1.0 + jnp.exp(-jnp.abs(z)))) * (1.0 / GLA_TAU) for z in zs]
        bs = [_chunk_cumsum(lg, L) for lg in loggs]
        ebs = [jnp.exp(b) for b in bs]
        enbs = [jnp.exp(-b) for b in bs]
        ebls = [[jnp.exp(b[c * L + L - 1:c * L + L, :]) for c in range(nc)] for b in bs]
        ebl_rows = [[jnp.broadcast_to(e, (L, GLA_DK)) for e in ebl] for ebl in ebls]
        ebl_rows = [r[0] if nc == 1 else jnp.concatenate(r, axis=0) for r in ebl_rows]
        qks = [_dot(hb, wqk_ref[:, 2 * h * GLA_DK:2 * (h + 1) * GLA_DK]) for h in hs]
        qds = [(qk[:, :GLA_DK] * (eb * GLA_DK ** -0.5)).astype(BF16) for qk, eb in zip(qks, ebs)]
        kdfs = [qk[:, GLA_DK:] * enb for qk, enb in zip(qks, enbs)]
        kds = [kdf.astype(BF16) for kdf in kdfs]
        kdecs = [kdf * r for kdf, r in zip(kdfs, ebl_rows)]
        vs = [_dot(hb, wv_ref[:, vc]).astype(BF16) for vc in vcs]
        gates = [_silu(_dot(hb, wr_ref[:, vc])) for vc in vcs]
        Ss = [s_ref[h] for h in hs]
        o_blocks = [[] for _ in hs]
        for r0 in range(0, tm, rb):
            rows = slice(r0, r0 + rb)
            vhs = [_pad_rows(v[rows], LANES) for v in vs]
            atts = [jnp.where(mask, _dot_nt(qd[rows], kd[rows]), 0.0).astype(BF16) for qd, kd in zip(qds, kds)]
            os_ = [_dot(att, vh[:rb]) for att, vh in zip(atts, vhs)]
            kdec_ts = [_pad_rows(kdec[rows], LANES).T for kdec in kdecs]
            lhss = [[jnp.where(lane_chunk == c, kdec_t, 0.0) for c in range(ncb)] for kdec_t in kdec_ts]
            lhss = [(l[0] if ncb == 1 else jnp.concatenate(l, axis=0)).astype(BF16) for l in lhss]
            us = [_dot(lhs, vh) for lhs, vh in zip(lhss, vhs)]
            o_inters = [[] for _ in hs]
            for c in range(ncb):
                for i in range(len(hs)):
                    o_inters[i].append(_dot(qds[i][r0 + c * L:r0 + (c + 1) * L], Ss[i].astype(BF16)))
                dcols = [jnp.broadcast_to(ebl[r0 // L + c], (GLA_DK, GLA_DK)).T for ebl in ebls]
                Ss = [S * jnp.concatenate([dcol, dcol], axis=1) + u[c * GLA_DK:(c + 1) * GLA_DK]
                      for S, dcol, u in zip(Ss, dcols, us)]
            for i in range(len(hs)):
                oi = o_inters[i]
                o_blocks[i].append(os_[i] + (oi[0] if ncb == 1 else jnp.concatenate(oi, axis=0)))
        for i, h in enumerate(hs):
            s_ref[h] = Ss[i]
            o = o_blocks[i][0] if len(o_blocks[i]) == 1 else jnp.concatenate(o_blocks[i], axis=0)
            o = o * lax.rsqrt(jnp.mean(o * o, axis=-1, keepdims=True) + NORM_EPS) * gn
            o_heads.append((o * gates[i]).astype(BF16))
    y_ref[...] = x + mod[2:3] * _dot(jnp.concatenate(o_heads, axis=1), wo_ref[...])


def _gla_layer(x, mod, ng, w, s0, tm, L):
    B, T, D = x.shape
    has_state = s0 is not None
    wqk, wv, wg1, wg2, bg, wr, gn, wo = w
    state_spec = pl.BlockSpec((None, GLA_HEADS, GLA_DK, GLA_DV), lambda b, t: (b, 0, 0, 0))
    in_specs = [
        pl.BlockSpec((None, tm, D), lambda b, t: (b, t, 0)),
        pl.BlockSpec((None, ADA_CHUNKS, D), lambda b, t: (b, 0, 0)),
        _resident((1, D)),
        _resident(wqk.shape), _resident(wv.shape), _resident(wg1.shape),
        _resident(wg2.shape), _resident(bg.shape), _resident(wr.shape), _resident(gn.shape), _resident(wo.shape),
    ]
    args = [x, mod, ng, wqk, wv, wg1, wg2, bg, wr, gn, wo]
    if has_state:
        in_specs.append(state_spec)
        args.append(s0)
    return pl.pallas_call(
        functools.partial(_gla_kernel, L=L, has_state=has_state),
        grid=(B, T // tm),
        in_specs=in_specs,
        out_specs=[pl.BlockSpec((None, tm, D), lambda b, t: (b, t, 0)), state_spec],
        out_shape=[jax.ShapeDtypeStruct((B, T, D), F32),
                   jax.ShapeDtypeStruct((B, GLA_HEADS, GLA_DK, GLA_DV), F32)],
        compiler_params=_params(("arbitrary", "arbitrary")),
        name="gla_layer",
    )(*args)


def _ret_kernel(*refs, has_state):
    if has_state:
        (x_ref, mod_ref, ng_ref, wq_ref, wk_ref, wv_ref, wg_ref, wo_ref, s0_ref, y_ref, s_ref,
         dintra_ref, dquery_ref, dkey_ref) = refs
    else:
        (x_ref, mod_ref, ng_ref, wq_ref, wk_ref, wv_ref, wg_ref, wo_ref, y_ref, s_ref,
         dintra_ref, dquery_ref, dkey_ref) = refs
    tm = x_ref.shape[0]
    L = min(tm, RET_RB)
    log_decay = [math.log(1.0 - 2.0 ** (-5.0 - h)) for h in range(RET_HEADS)]

    @pl.when((pl.program_id(0) == 0) & (pl.program_id(1) == 0))
    def _():
        rel = (lax.broadcasted_iota(jnp.int32, (L, L), 0) - lax.broadcasted_iota(jnp.int32, (L, L), 1)).astype(F32)
        pos = lax.broadcasted_iota(jnp.int32, (L, LANES), 0).astype(F32)
        for h in range(RET_HEADS):
            dintra_ref[h] = jnp.where(rel >= 0.0, jnp.exp(log_decay[h] * jnp.maximum(rel, 0.0)), 0.0)
            dquery_ref[h] = jnp.exp(log_decay[h] * (pos + 1.0))
            dkey_ref[h] = jnp.exp(log_decay[h] * (L - 1.0 - pos))

    @pl.when(pl.program_id(1) == 0)
    def _():
        if has_state:
            s_ref[...] = s0_ref[...]
        else:
            s_ref[...] = jnp.zeros(s_ref.shape, F32)

    x = x_ref[...]
    mod = mod_ref[...]
    hb = _modulated_norm(x, ng_ref[...], mod[0:1], mod[1:2]).astype(BF16)
    q = _dot(hb, wq_ref[...])
    k = _dot(hb, wk_ref[...]) * (RET_DK ** -0.5)
    v = _dot(hb, wv_ref[...])
    g = _dot(hb, wg_ref[...])
    qb = q.astype(BF16)
    kb = k.astype(BF16)
    vb = _pad_rows(v.astype(BF16), LANES)

    o_heads = []
    for h0 in range(0, RET_HEADS, RET_LOCKSTEP):
        hs = list(range(h0, min(h0 + RET_LOCKSTEP, RET_HEADS)))
        kcs = [slice(h * RET_DK, (h + 1) * RET_DK) for h in hs]
        Ss = [s_ref[h] for h in hs]
        o_blocks = [[] for _ in hs]
        for r0 in range(0, tm, L):
            rows = slice(r0, r0 + L)
            vhs = [vb[r0:r0 + vb.shape[0] - tm + L, h * RET_DV:(h + 1) * RET_DV] for h in hs]
            atts = [(_dot_nt(qb[rows, kc], kb[rows, kc]) * dintra_ref[h]).astype(BF16) for h, kc in zip(hs, kcs)]
            os_ = [_dot(att, vh[:L]) + _dot(qb[rows, kc], S.astype(BF16))
                   * jnp.concatenate([dquery_ref[h]] * (RET_DV // LANES), axis=1)
                   for att, vh, kc, S, h in zip(atts, vhs, kcs, Ss, hs)]
            k_decs = [_pad_rows(k[rows, kc] * jnp.concatenate([dkey_ref[h]] * (RET_DK // LANES), axis=1), LANES)
                      for h, kc in zip(hs, kcs)]
            Ss = [S * math.exp(log_decay[h] * L) + _dot(k_dec.T.astype(BF16), vh)
                  for h, S, k_dec, vh in zip(hs, Ss, k_decs, vhs)]
            ocs = [o - jnp.mean(o, axis=-1, keepdims=True) for o in os_]
            for i, oc in enumerate(ocs):
                o_blocks[i].append(oc * lax.rsqrt(jnp.mean(oc * oc, axis=-1, keepdims=True) + GN_EPS))
        for i, h in enumerate(hs):
            s_ref[h] = Ss[i]
            o_heads.append(o_blocks[i][0] if len(o_blocks[i]) == 1 else jnp.concatenate(o_blocks[i], axis=0))
    o = jnp.concatenate(o_heads, axis=1)
    o = (o * _silu(g)).astype(BF16)
    y_ref[...] = x + mod[2:3] * _dot(o, wo_ref[...])


def _ret_layer(x, mod, ng, w, s0, tm):
    B, T, D = x.shape
    has_state = s0 is not None
    wq, wk, wv, wg, wo = w
    state_spec = pl.BlockSpec((None, RET_HEADS, RET_DK, RET_DV), lambda b, t: (b, 0, 0, 0))
    in_specs = [
        pl.BlockSpec((None, tm, D), lambda b, t: (b, t, 0)),
        pl.BlockSpec((None, ADA_CHUNKS, D), lambda b, t: (b, 0, 0)),
        _resident((1, D)),
        _resident(wq.shape), _resident(wk.shape), _resident(wv.shape), _resident(wg.shape), _resident(wo.shape),
    ]
    args = [x, mod, ng, wq, wk, wv, wg, wo]
    if has_state:
        in_specs.append(state_spec)
        args.append(s0)
    return pl.pallas_call(
        functools.partial(_ret_kernel, has_state=has_state),
        grid=(B, T // tm),
        in_specs=in_specs,
        out_specs=[pl.BlockSpec((None, tm, D), lambda b, t: (b, t, 0)), state_spec],
        out_shape=[jax.ShapeDtypeStruct((B, T, D), F32),
                   jax.ShapeDtypeStruct((B, RET_HEADS, RET_DK, RET_DV), F32)],
        scratch_shapes=[pltpu.VMEM((RET_HEADS, min(tm, RET_RB), min(tm, RET_RB)), F32),
                        pltpu.VMEM((RET_HEADS, min(tm, RET_RB), LANES), F32),
                        pltpu.VMEM((RET_HEADS, min(tm, RET_RB), LANES), F32)],
        compiler_params=_params(("arbitrary", "arbitrary")),
        name="ret_layer",
    )(*args)


def _head_mean_square(a, hsum):
    w = hsum.shape[0]
    parts = [_dot((a[:, j:j + w] * a[:, j:j + w]).astype(BF16), hsum) for j in range(0, a.shape[1], w)]
    return parts[0] if len(parts) == 1 else jnp.concatenate(parts, axis=1)


def _tile_head_lanes(a, kv):
    t = a[:, (kv // 2) * LANES:(kv // 2 + 1) * LANES]
    low = lax.broadcasted_iota(jnp.int32, t.shape, 1) < SWA_HD
    keep = low if kv % 2 == 0 else jnp.logical_not(low)
    return jnp.where(keep, t, pltpu.roll(t, SWA_HD, axis=1)).astype(BF16)


def _swa_kernel(*refs, Lq, is_prompt):
    if is_prompt:
        (sink_ref, x_ref, mod_ref, ng_ref, wqkv_ref, qn_ref, kn_ref, wo_ref, y_ref, kc_ref, vc_ref,
         bias_ref) = refs
    else:
        (sink_ref, x_ref, mod_ref, ng_ref, wqkv_ref, qn_ref, kn_ref, wo_ref, ck_ref, cv_ref,
         y_ref, kc_ref, vc_ref, bias_ref) = refs
    t = pl.program_id(1)
    x = x_ref[...]
    tm = x.shape[0]
    nq = SWA_Q_HEADS * SWA_HD
    nk = SWA_KV_HEADS * SWA_HD
    gw = SWA_GROUP * SWA_HD
    nkeys = WINDOW + Lq
    nkp = nkeys + 1 + (-(nkeys + 1)) % LANES
    rows_q = SWA_GROUP * Lq

    @pl.when((pl.program_id(0) == 0) & (t == 0))
    def _():
        ri = lax.broadcasted_iota(jnp.int32, (rows_q, nkp), 0)
        ji = lax.broadcasted_iota(jnp.int32, (rows_q, nkp), 1)
        grp = ri // Lq
        dist = jnp.abs((ri - grp * Lq) - ji + WINDOW).astype(F32)
        for kv in range(SWA_KV_HEADS):
            sink = jnp.zeros((rows_q, nkp), F32)
            for gi in range(SWA_GROUP):
                sink = jnp.where(grp == gi, sink_ref[kv * SWA_GROUP + gi] * LOG2E, sink)
            bias = -LOG2E * jnp.exp2(-0.5 * (kv * SWA_GROUP + grp + 1).astype(F32)) * dist
            bias_ref[kv] = jnp.where(ji < nkeys, bias, jnp.where(ji == nkeys, sink, NEG_INF))

    mod = mod_ref[...]
    hb = _modulated_norm(x, ng_ref[...], mod[0:1], mod[1:2]).astype(BF16)
    qkv = _dot(hb, wqkv_ref[...])
    hi = lax.broadcasted_iota(jnp.int32, (gw, gw), 0) // SWA_HD
    hj = lax.broadcasted_iota(jnp.int32, (gw, gw), 1) // SWA_HD
    hsum = jnp.where(hi == hj, 1.0 / SWA_HD, 0.0).astype(BF16)
    q = qkv[:, :nq]
    k = qkv[:, nq:nq + nk]
    q = (q * lax.rsqrt(_head_mean_square(q, hsum) + NORM_EPS) * (qn_ref[...] * (SWA_HD ** -0.5 * LOG2E))).astype(BF16)
    k_new = k * lax.rsqrt(_head_mean_square(k, hsum) + NORM_EPS) * kn_ref[...]
    v_new = qkv[:, nq + nk:]

    if is_prompt:
        @pl.when(t == 0)
        def _():
            kc_ref[...] = jnp.zeros(kc_ref.shape, F32)
            vc_ref[...] = jnp.zeros(vc_ref.shape, F32)
        k_prev = kc_ref[...]
        v_prev = vc_ref[...]
    else:
        k_prev = ck_ref[...]
        v_prev = cv_ref[...]
    k_ext = jnp.concatenate([k_prev, k_new, jnp.zeros((nkp - nkeys, nk), F32)], axis=0)
    v_ext = jnp.concatenate([v_prev, v_new], axis=0)

    low_half = lax.broadcasted_iota(jnp.int32, (Lq, LANES), 1) < SWA_HD
    ji = lax.broadcasted_iota(jnp.int32, (rows_q, nkp), 1)
    is_key = ji < nkeys
    zero = jnp.zeros((), BF16)

    o_cols = []
    for kv in range(SWA_KV_HEADS):
        kt = _tile_head_lanes(k_ext, kv)
        vt = _tile_head_lanes(v_ext, kv)
        bias = bias_ref[kv]
        o_rows = []
        nchunk = tm // Lq
        for c0 in range(0, nchunk, SWA_LOCKSTEP):
            r0s = [c * Lq for c in range(c0, min(c0 + SWA_LOCKSTEP, nchunk))]
            qss = []
            for r0 in r0s:
                qs = []
                for gi in range(SWA_GROUP):
                    qt = q[r0:r0 + Lq, kv * gw + (gi // 2) * LANES:kv * gw + (gi // 2 + 1) * LANES]
                    qs.append(jnp.where(low_half if gi % 2 == 0 else jnp.logical_not(low_half), qt, zero))
                qss.append(jnp.concatenate(qs, axis=0))
            logits = [jnp.where(is_key, _dot_nt(qs, kt[r0:r0 + nkp]), 0.0) + bias for qs, r0 in zip(qss, r0s)]
            if is_prompt:
                logits = [jnp.where(ji >= WINDOW - r0 - t * tm, lg, NEG_INF) if r0 < WINDOW else lg
                          for lg, r0 in zip(logits, r0s)]
            ps = [jnp.exp2(lg - jnp.max(lg, axis=-1, keepdims=True)) for lg in logits]
            dens = [jnp.sum(p, axis=-1, keepdims=True) for p in ps]
            ocs = [_dot(p[:, :nkeys].astype(BF16), vt[r0:r0 + nkeys]) * (1.0 / den)
                   for p, r0, den in zip(ps, r0s, dens)]
            for oc in ocs:
                o_rows.append(jnp.concatenate(
                    [jnp.where(low_half, oc[2 * j * Lq:(2 * j + 1) * Lq], oc[(2 * j + 1) * Lq:(2 * j + 2) * Lq])
                     for j in range(SWA_GROUP // 2)], axis=1))
        o_cols.append(o_rows[0] if len(o_rows) == 1 else jnp.concatenate(o_rows, axis=0))
    o = jnp.concatenate(o_cols, axis=1)
    y_ref[...] = x + mod[2:3] * _dot(o.astype(BF16), wo_ref[...])
    if is_prompt:
        kc_ref[...] = k_new[tm - WINDOW:]
        vc_ref[...] = v_new[tm - WINDOW:]
    else:
        kc_ref[...] = k_new
        vc_ref[...] = v_new


def _swa_layer(x, mod, ng, w, cache, tm, Lq):
    B, T, D = x.shape
    is_prompt = cache is None
    wqkv, qn, kn, sinks, wo = w
    nk = SWA_KV_HEADS * SWA_HD
    in_specs = [
        pl.BlockSpec(memory_space=pltpu.SMEM),
        pl.BlockSpec((None, tm, D), lambda b, t: (b, t, 0)),
        pl.BlockSpec((None, ADA_CHUNKS, D), lambda b, t: (b, 0, 0)),
        _resident((1, D)),
        _resident(wqkv.shape), _resident(qn.shape), _resident(kn.shape), _resident(wo.shape),
    ]
    args = [sinks, x, mod, ng, wqkv, qn, kn, wo]
    if is_prompt:
        rows_out = WINDOW
    else:
        ck, cv = cache
        in_specs += [pl.BlockSpec((None, WINDOW, nk), lambda b, t: (b, 0, 0))] * 2
        args += [ck.reshape(B, WINDOW, nk), cv.reshape(B, WINDOW, nk)]
        rows_out = T
    cache_spec = pl.BlockSpec((None, rows_out, nk), lambda b, t: (b, 0, 0))
    y, kc, vc = pl.pallas_call(
        functools.partial(_swa_kernel, Lq=Lq, is_prompt=is_prompt),
        grid=(B, T // tm),
        in_specs=in_specs,
        out_specs=[pl.BlockSpec((None, tm, D), lambda b, t: (b, t, 0)), cache_spec, cache_spec],
        out_shape=[jax.ShapeDtypeStruct((B, T, D), F32),
                   jax.ShapeDtypeStruct((B, rows_out, nk), F32),
                   jax.ShapeDtypeStruct((B, rows_out, nk), F32)],
        scratch_shapes=[pltpu.VMEM((SWA_KV_HEADS, SWA_GROUP * Lq, WINDOW + Lq + 1 + (-(WINDOW + Lq + 1)) % LANES), F32)],
        compiler_params=_params(("arbitrary", "arbitrary")),
        name="swa_layer",
    )(*args)
    return (y, kc.reshape(B, rows_out, SWA_KV_HEADS, SWA_HD), vc.reshape(B, rows_out, SWA_KV_HEADS, SWA_HD))


def _tile(T, pref):
    return pref if T % pref == 0 else T


def kernel(x_prompt, x_sample, state_gla_l0, state_ret_l1, cache_swa_k_l2, cache_swa_v_l2, state_gla_l3, c_prompt, c_sample, norm_mix, norm_mlp, ada_w, ada_b, mlp_w1, mlp_w2, gla_wq, gla_wk, gla_wv, gla_wg1, gla_wg2, gla_bg, gla_wr, gla_norm, gla_wo, ret_wq, ret_wk, ret_wv, ret_wg, ret_wo, swa_wqkv, swa_qnorm, swa_knorm, swa_sinks, swa_wo):
    bp = x_prompt.shape[0]
    bs = x_sample.shape[0]
    Tp = x_prompt.shape[1]
    Ts = x_sample.shape[1]
    gla_states = [state_gla_l0, state_gla_l3]
    ret_states = [state_ret_l1]
    swa_caches = [(cache_swa_k_l2, cache_swa_v_l2)]

    mod_all = _ada_modulation(jnp.concatenate([c_prompt, c_sample], axis=0), ada_w, ada_b)
    mod_all = mod_all.reshape(DEPTH, bp + bs, ADA_CHUNKS, D_MODEL)

    yp, ys = x_prompt, x_sample
    new_state = []
    for layer in range(DEPTH):
        kind, j = layer % N_MIXERS, layer // N_MIXERS
        mod_p = mod_all[layer, :bp]
        mod_s = mod_all[layer, bp:]
        ng = norm_mix[layer].reshape(1, D_MODEL)
        if kind == 0:
            wqk = jnp.concatenate([gla_wq[j].reshape(D_MODEL, GLA_HEADS, GLA_DK),
                                   gla_wk[j].reshape(D_MODEL, GLA_HEADS, GLA_DK)], axis=2).reshape(D_MODEL, -1)
            w = (wqk.astype(BF16), gla_wv[j].astype(BF16), gla_wg1[j].astype(BF16),
                 gla_wg2[j].astype(BF16), gla_bg[j].reshape(1, -1), gla_wr[j].astype(BF16),
                 gla_norm[j].reshape(1, -1), gla_wo[j].astype(BF16))
            yp, st_p = _gla_layer(yp, mod_p, ng, w, None, _tile(Tp, 512), min(CHUNK, Tp))
            ys, st_s = _gla_layer(ys, mod_s, ng, w, gla_states[j], Ts, min(CHUNK, Ts))
            new_state += [st_p, st_s]
        elif kind == 1:
            w = (ret_wq[j].astype(BF16), ret_wk[j].astype(BF16), ret_wv[j].astype(BF16), ret_wg[j].astype(BF16),
                 ret_wo[j].astype(BF16))
            yp, st_p = _ret_layer(yp, mod_p, ng, w, None, _tile(Tp, 512))
            ys, st_s = _ret_layer(ys, mod_s, ng, w, ret_states[j], Ts)
            new_state += [st_p, st_s]
        else:
            w = (swa_wqkv[j].astype(BF16), jnp.tile(swa_qnorm[j], SWA_Q_HEADS).reshape(1, -1),
                 jnp.tile(swa_knorm[j], SWA_KV_HEADS).reshape(1, -1), swa_sinks[j], swa_wo[j].astype(BF16))
            yp, k_p, v_p = _swa_layer(yp, mod_p, ng, w, None, _tile(Tp, 512), CHUNK)
            ys, k_s, v_s = _swa_layer(ys, mod_s, ng, w, swa_caches[j], Ts, Ts)
            new_state += [k_p, v_p, k_s, v_s]
        gm = norm_mlp[layer].reshape(1, D_MODEL)
        w1 = mlp_w1[layer].astype(BF16)
        w2 = mlp_w2[layer].astype(BF16)
        yp = _mlp_layer(yp, mod_p[:, :, None, :], gm, w1, w2, _tile(Tp, 1024))
        mod_rows = jnp.repeat(mod_s.transpose(1, 0, 2), Ts, axis=1)[None]
        ys = _mlp_layer(ys.reshape(1, bs * Ts, D_MODEL), mod_rows, gm, w1, w2, bs * Ts).reshape(bs, Ts, D_MODEL)
    return (yp, ys, *new_state)
```

```python
import functools
import math

import jax
import jax.numpy as jnp
from jax import lax
from jax.experimental import pallas as pl
from jax.experimental.pallas import tpu as pltpu

F32 = jnp.float32
BF16 = jnp.bfloat16

D_MODEL = 1024
DEPTH = 4
CHUNK = 64
N_MIXERS = 3
GLA_HEADS = 4
GLA_DK = 128
GLA_DV = 256
GLA_RANK = 16
GLA_TAU = 16.0
GLA_RB = 256
GLA_LOCKSTEP = 2
RET_HEADS = 4
RET_DK = 256
RET_DV = 512
RET_RB = 256
RET_LOCKSTEP = 2
SWA_Q_HEADS = 16
SWA_KV_HEADS = 4
SWA_GROUP = 4
SWA_HD = 64
WINDOW = 128
SWA_LOCKSTEP = 4
MLP_HIDDEN = 4 * D_MODEL
ADA_CHUNKS = 6
NORM_EPS = 1e-6
GN_EPS = 1e-5
NEG_INF = -1e30
LOG2E = 1.4426950408889634

VMEM_LIMIT_BYTES = 56 * 1024 * 1024
LANES = 128


def _dot(a, b):
    return jnp.dot(a, b, preferred_element_type=F32)


def _dot_nt(a, b):
    return lax.dot_general(a, b, (((1,), (1,)), ((), ())), preferred_element_type=F32)


def _pad_rows(a, mult):
    r = (-a.shape[0]) % mult
    return a if r == 0 else jnp.concatenate([a, jnp.zeros((r, a.shape[1]), a.dtype)], axis=0)


def _modulated_norm(x, g, shift, scale):
    ms = jnp.mean(x * x, axis=-1, keepdims=True)
    return x * lax.rsqrt(ms + NORM_EPS) * (g * (1.0 + scale)) + shift


def _silu(x):
    return x * (1.0 / (1.0 + jnp.exp2(x * -LOG2E)))


def _resident(shape):
    nd = len(shape)
    return pl.BlockSpec(shape, lambda *_: (0,) * nd, pipeline_mode=pl.Buffered(1))


def _params(sem):
    return pltpu.CompilerParams(dimension_semantics=sem, vmem_limit_bytes=VMEM_LIMIT_BYTES)


def _ada_kernel(c_ref, w_ref, b_ref, o_ref):
    c = c_ref[...]
    a = _silu(c).astype(BF16)
    o_ref[...] = _dot(a, w_ref[...].astype(BF16)) + b_ref[...]


def _ada_modulation(c_all, ada_w, ada_b):
    nb = c_all.shape[0]
    width = ADA_CHUNKS * D_MODEL
    tn = 1536
    return pl.pallas_call(
        _ada_kernel,
        grid=(DEPTH, width // tn),
        in_specs=[
            pl.BlockSpec((nb, D_MODEL), lambda l, j: (0, 0)),
            pl.BlockSpec((None, D_MODEL, tn), lambda l, j: (l, 0, j)),
            pl.BlockSpec((None, 1, tn), lambda l, j: (l, 0, j)),
        ],
        out_specs=pl.BlockSpec((None, nb, tn), lambda l, j: (l, 0, j)),
        out_shape=jax.ShapeDtypeStruct((DEPTH, nb, width), F32),
        compiler_params=_params(("arbitrary", "arbitrary")),
        name="ada_modulation",
    )(c_all, ada_w, ada_b.reshape(DEPTH, 1, width))


def _mlp_kernel(x_ref, mod_ref, g_ref, w1_ref, w2_ref, o_ref, *, th):
    x = x_ref[...]
    h = _modulated_norm(x, g_ref[...], mod_ref[3], mod_ref[4]).astype(BF16)
    acc = jnp.zeros(x.shape, F32)
    for c in range(MLP_HIDDEN // th):
        a = _dot(h, w1_ref[:, c * th:(c + 1) * th])
        a = jnp.maximum(a, 0.0)
        a = a * a
        acc = acc + _dot(a.astype(BF16), w2_ref[c * th:(c + 1) * th, :])
    o_ref[...] = x + mod_ref[5] * acc


def _mlp_layer(x, mod, g, w1_all, w2_all, layer, tm):
    B, T, D = x.shape
    return pl.pallas_call(
        functools.partial(_mlp_kernel, th=512),
        grid=(B, T // tm),
        in_specs=[
            pl.BlockSpec((None, tm, D), lambda b, t: (b, t, 0)),
            pl.BlockSpec((None, ADA_CHUNKS, mod.shape[2], D), lambda b, t: (b, 0, 0, 0)),
            _resident((1, D)),
            pl.BlockSpec((None, D, MLP_HIDDEN), lambda b, t: (layer, 0, 0), pipeline_mode=pl.Buffered(1)),
            pl.BlockSpec((None, MLP_HIDDEN, D), lambda b, t: (layer, 0, 0), pipeline_mode=pl.Buffered(1)),
        ],
        out_specs=pl.BlockSpec((None, tm, D), lambda b, t: (b, t, 0)),
        out_shape=jax.ShapeDtypeStruct((B, T, D), F32),
        compiler_params=_params(("arbitrary", "arbitrary")),
        name="mlp_layer",
    )(x, mod, g, w1_all, w2_all)


def _chunk_cumsum(a, L):
    row = lax.broadcasted_iota(jnp.int32, a.shape, 0) % L
    s = 1
    while s < L:
        a = a + jnp.where(row >= s, pltpu.roll(a, s, axis=0), 0.0)
        s *= 2
    return a


def _gla_kernel(*refs, L, has_state):
    if has_state:
        (x_ref, mod_ref, ng_ref, wqk_ref, wv_ref, wg1_ref, wg2_ref, bg_ref, wr_ref, gn_ref, wo_ref,
         s0_ref, y_ref, s_ref) = refs
    else:
        (x_ref, mod_ref, ng_ref, wqk_ref, wv_ref, wg1_ref, wg2_ref, bg_ref, wr_ref, gn_ref, wo_ref,
         y_ref, s_ref) = refs

    @pl.when(pl.program_id(1) == 0)
    def _():
        if has_state:
            s_ref[...] = s0_ref[...]
        else:
            s_ref[...] = jnp.zeros(s_ref.shape, F32)

    x = x_ref[...]
    tm = x.shape[0]
    nc = tm // L
    mod = mod_ref[...]
    hb = _modulated_norm(x, ng_ref[...], mod[0:1], mod[1:2]).astype(BF16)
    g1 = _dot(hb, wg1_ref[...]).astype(BF16)
    rb = min(tm, GLA_RB)
    ncb = rb // L
    ri = lax.broadcasted_iota(jnp.int32, (rb, rb), 0)
    ci = lax.broadcasted_iota(jnp.int32, (rb, rb), 1)
    mask = ri >= ci
    if ncb > 1:
        mask = mask & (ci >= (ri // L) * L)
    rbp = rb + (-rb) % LANES
    lane_chunk = lax.broadcasted_iota(jnp.int32, (GLA_DK, rbp), 1) // L
    gn = gn_ref[...]

    o_heads = []
    for h0 in range(0, GLA_HEADS, GLA_LOCKSTEP):
        hs = list(range(h0, min(h0 + GLA_LOCKSTEP, GLA_HEADS)))
        kcs = [slice(h * GLA_DK, (h + 1) * GLA_DK) for h in hs]
        vcs = [slice(h * GLA_DV, (h + 1) * GLA_DV) for h in hs]
        zs = [_dot(g1, wg2_ref[:, kc]) + bg_ref[:, kc] for kc in kcs]
        loggs = [(jnp.minimum(z, 0.0) - jnp.log(1.0 + jnp.exp(-jnp.abs(z)))) * (1.0 / GLA_TAU) for z in zs]
        bs = [_chunk_cumsum(lg, L) for lg in loggs]
        ebs = [jnp.exp(b) for b in bs]
        enbs = [jnp.exp(-b) for b in bs]
        ebls = [[jnp.exp(b[c * L + L - 1:c * L + L, :]) for c in range(nc)] for b in bs]
        ebl_rows = [[jnp.broadcast_to(e, (L, GLA_DK)) for e in ebl] for ebl in ebls]
        ebl_rows = [r[0] if nc == 1 else jnp.concatenate(r, axis=0) for r in ebl_rows]
        qks = [_dot(hb, wqk_ref[:, 2 * h * GLA_DK:2 * (h + 1) * GLA_DK]) for h in hs]
        qds = [(qk[:, :GLA_DK] * (eb * GLA_DK ** -0.5)).astype(BF16) for qk, eb in zip(qks, ebs)]
        kdfs = [qk[:, GLA_DK:] * enb for qk, enb in zip(qks, enbs)]
        kds = [kdf.astype(BF16) for kdf in kdfs]
        kdecs = [kdf * r for kdf, r in zip(kdfs, ebl_rows)]
        vs = [_dot(hb, wv_ref[:, vc]).astype(BF16) for vc in vcs]
        gates = [_silu(_dot(hb, wr_ref[:, vc])) for vc in vcs]
        Ss = [s_ref[h] for h in hs]
        o_blocks = [[] for _ in hs]
        for r0 in range(0, tm, rb):
            rows = slice(r0, r0 + rb)
            vhs = [_pad_rows(v[rows], LANES) for v in vs]
            atts = [jnp.where(mask, _dot_nt(qd[rows], kd[rows]), 0.0).astype(BF16) for qd, kd in zip(qds, kds)]
            os_ = [_dot(att, vh[:rb]) for att, vh in zip(atts, vhs)]
            kdec_ts = [_pad_rows(kdec[rows], LANES).T for kdec in kdecs]
            lhss = [[jnp.where(lane_chunk == c, kdec_t, 0.0) for c in range(ncb)] for kdec_t in kdec_ts]
            lhss = [(l[0] if ncb == 1 else jnp.concatenate(l, axis=0)).astype(BF16) for l in lhss]
            us = [_dot(lhs, vh) for lhs, vh in zip(lhss, vhs)]
            o_inters = [[] for _ in hs]
            for c in range(ncb):
                for i in range(len(hs)):
                    o_inters[i].append(_dot(qds[i][r0 + c * L:r0 + (c + 1) * L], Ss[i].astype(BF16)))
                dcols = [jnp.broadcast_to(ebl[r0 // L + c], (GLA_DK, GLA_DK)).T for ebl in ebls]
                Ss = [S * jnp.concatenate([dcol, dcol], axis=1) + u[c * GLA_DK:(c + 1) * GLA_DK]
                      for S, dcol, u in zip(Ss, dcols, us)]
            for i in range(len(hs)):
                oi = o_inters[i]
                o_blocks[i].append(os_[i] + (oi[0] if ncb == 1 else jnp.concatenate(oi, axis=0)))
        for i, h in enumerate(hs):
            s_ref[h] = Ss[i]
            o = o_blocks[i][0] if len(o_blocks[i]) == 1 else jnp.concatenate(o_blocks[i], axis=0)
            o = o * lax.rsqrt(jnp.mean(o * o, axis=-1, keepdims=True) + NORM_EPS) * gn
            o_heads.append((o * gates[i]).astype(BF16))
    y_ref[...] = x + mod[2:3] * _dot(jnp.concatenate(o_heads, axis=1), wo_ref[...])


def _gla_layer(x, mod, ng, w, s0, tm, L):
    B, T, D = x.shape
    has_state = s0 is not None
    wqk, wv, wg1, wg2, bg, wr, gn, wo = w
    state_spec = pl.BlockSpec((None, GLA_HEADS, GLA_DK, GLA_DV), lambda b, t: (b, 0, 0, 0))
    in_specs = [
        pl.BlockSpec((None, tm, D), lambda b, t: (b, t, 0)),
        pl.BlockSpec((None, ADA_CHUNKS, D), lambda b, t: (b, 0, 0)),
        _resident((1, D)),
        _resident(wqk.shape), _resident(wv.shape), _resident(wg1.shape),
        _resident(wg2.shape), _resident(bg.shape), _resident(wr.shape), _resident(gn.shape), _resident(wo.shape),
    ]
    args = [x, mod, ng, wqk, wv, wg1, wg2, bg, wr, gn, wo]
    if has_state:
        in_specs.append(state_spec)
        args.append(s0)
    return pl.pallas_call(
        functools.partial(_gla_kernel, L=L, has_state=has_state),
        grid=(B, T // tm),
        in_specs=in_specs,
        out_specs=[pl.BlockSpec((None, tm, D), lambda b, t: (b, t, 0)), state_spec],
        out_shape=[jax.ShapeDtypeStruct((B, T, D), F32),
                   jax.ShapeDtypeStruct((B, GLA_HEADS, GLA_DK, GLA_DV), F32)],
        compiler_params=_params(("arbitrary", "arbitrary")),
        name="gla_layer",
    )(*args)


def _ret_kernel(*refs, has_state):
    if has_state:
        (x_ref, mod_ref, ng_ref, wq_ref, wk_ref, wv_ref, wg_ref, wo_ref, s0_ref, y_ref, s_ref,
         dintra_ref, dquery_ref, dkey_ref) = refs
    else:
        (x_ref, mod_ref, ng_ref, wq_ref, wk_ref, wv_ref, wg_ref, wo_ref, y_ref, s_ref,
         dintra_ref, dquery_ref, dkey_ref) = refs
    tm = x_ref.shape[0]
    L = min(tm, RET_RB)
    log_decay = [math.log(1.0 - 2.0 ** (-5.0 - h)) for h in range(RET_HEADS)]

    @pl.when((pl.program_id(0) == 0) & (pl.program_id(1) == 0))
    def _():
        rel = (lax.broadcasted_iota(jnp.int32, (L, L), 0) - lax.broadcasted_iota(jnp.int32, (L, L), 1)).astype(F32)
        pos = lax.broadcasted_iota(jnp.int32, (L, LANES), 0).astype(F32)
        for h in range(RET_HEADS):
            dintra_ref[h] = jnp.where(rel >= 0.0, jnp.exp(log_decay[h] * jnp.maximum(rel, 0.0)), 0.0)
            dquery_ref[h] = jnp.exp(log_decay[h] * (pos + 1.0))
            dkey_ref[h] = jnp.exp(log_decay[h] * (L - 1.0 - pos))

    @pl.when(pl.program_id(1) == 0)
    def _():
        if has_state:
            s_ref[...] = s0_ref[...]
        else:
            s_ref[...] = jnp.zeros(s_ref.shape, F32)

    x = x_ref[...]
    mod = mod_ref[...]
    hb = _modulated_norm(x, ng_ref[...], mod[0:1], mod[1:2]).astype(BF16)
    q = _dot(hb, wq_ref[...])
    k = _dot(hb, wk_ref[...]) * (RET_DK ** -0.5)
    v = _dot(hb, wv_ref[...])
    g = _dot(hb, wg_ref[...])
    qb = q.astype(BF16)
    kb = k.astype(BF16)
    vb = _pad_rows(v.astype(BF16), LANES)

    o_heads = []
    for h0 in range(0, RET_HEADS, RET_LOCKSTEP):
        hs = list(range(h0, min(h0 + RET_LOCKSTEP, RET_HEADS)))
        kcs = [slice(h * RET_DK, (h + 1) * RET_DK) for h in hs]
        Ss = [s_ref[h] for h in hs]
        o_blocks = [[] for _ in hs]
        for r0 in range(0, tm, L):
            rows = slice(r0, r0 + L)
            vhs = [vb[r0:r0 + vb.shape[0] - tm + L, h * RET_DV:(h + 1) * RET_DV] for h in hs]
            atts = [(_dot_nt(qb[rows, kc], kb[rows, kc]) * dintra_ref[h]).astype(BF16) for h, kc in zip(hs, kcs)]
            os_ = [_dot(att, vh[:L]) + _dot(qb[rows, kc], S.astype(BF16))
                   * jnp.concatenate([dquery_ref[h]] * (RET_DV // LANES), axis=1)
                   for att, vh, kc, S, h in zip(atts, vhs, kcs, Ss, hs)]
            k_decs = [_pad_rows(k[rows, kc] * jnp.concatenate([dkey_ref[h]] * (RET_DK // LANES), axis=1), LANES)
                      for h, kc in zip(hs, kcs)]
            Ss = [S * math.exp(log_decay[h] * L) + _dot(k_dec.T.astype(BF16), vh)
                  for h, S, k_dec, vh in zip(hs, Ss, k_decs, vhs)]
            ocs = [o - jnp.mean(o, axis=-1, keepdims=True) for o in os_]
            for i, oc in enumerate(ocs):
                o_blocks[i].append(oc * lax.rsqrt(jnp.mean(oc * oc, axis=-1, keepdims=True) + GN_EPS))
        for i, h in enumerate(hs):
            s_ref[h] = Ss[i]
            o_heads.append(o_blocks[i][0] if len(o_blocks[i]) == 1 else jnp.concatenate(o_blocks[i], axis=0))
    o = jnp.concatenate(o_heads, axis=1)
    o = (o * _silu(g)).astype(BF16)
    y_ref[...] = x + mod[2:3] * _dot(o, wo_ref[...])


def _ret_layer(x, mod, ng, w, s0, tm):
    B, T, D = x.shape
    has_state = s0 is not None
    wq, wk, wv, wg, wo = w
    state_spec = pl.BlockSpec((None, RET_HEADS, RET_DK, RET_DV), lambda b, t: (b, 0, 0, 0))
    in_specs = [
        pl.BlockSpec((None, tm, D), lambda b, t: (b, t, 0)),
        pl.BlockSpec((None, ADA_CHUNKS, D), lambda b, t: (b, 0, 0)),
        _resident((1, D)),
        _resident(wq.shape), _resident(wk.shape), _resident(wv.shape), _resident(wg.shape), _resident(wo.shape),
    ]
    args = [x, mod, ng, wq, wk, wv, wg, wo]
    if has_state:
        in_specs.append(state_spec)
        args.append(s0)
    return pl.pallas_call(
        functools.partial(_ret_kernel, has_state=has_state),
        grid=(B, T // tm),
        in_specs=in_specs,
        out_specs=[pl.BlockSpec((None, tm, D), lambda b, t: (b, t, 0)), state_spec],
        out_shape=[jax.ShapeDtypeStruct((B, T, D), F32),
                   jax.ShapeDtypeStruct((B, RET_HEADS, RET_DK, RET_DV), F32)],
        scratch_shapes=[pltpu.VMEM((RET_HEADS, min(tm, RET_RB), min(tm, RET_RB)), F32),
                        pltpu.VMEM((RET_HEADS, min(tm, RET_RB), LANES), F32),
                        pltpu.VMEM((RET_HEADS, min(tm, RET_RB), LANES), F32)],
        compiler_params=_params(("arbitrary", "arbitrary")),
        name="ret_layer",
    )(*args)


def _head_mean_square(a, hsum):
    w = hsum.shape[0]
    parts = [_dot((a[:, j:j + w] * a[:, j:j + w]).astype(BF16), hsum) for j in range(0, a.shape[1], w)]
    return parts[0] if len(parts) == 1 else jnp.concatenate(parts, axis=1)


def _tile_head_lanes(a, kv):
    t = a[:, (kv // 2) * LANES:(kv // 2 + 1) * LANES]
    low = lax.broadcasted_iota(jnp.int32, t.shape, 1) < SWA_HD
    keep = low if kv % 2 == 0 else jnp.logical_not(low)
    return jnp.where(keep, t, pltpu.roll(t, SWA_HD, axis=1)).astype(BF16)


def _swa_kernel(*refs, Lq, is_prompt):
    if is_prompt:
        (sink_ref, x_ref, mod_ref, ng_ref, wqkv_ref, qn_ref, kn_ref, wo_ref, y_ref, kc_ref, vc_ref,
         bias_ref) = refs
    else:
        (sink_ref, x_ref, mod_ref, ng_ref, wqkv_ref, qn_ref, kn_ref, wo_ref, ck_ref, cv_ref,
         y_ref, kc_ref, vc_ref, bias_ref) = refs
    t = pl.program_id(1)
    x = x_ref[...]
    tm = x.shape[0]
    nq = SWA_Q_HEADS * SWA_HD
    nk = SWA_KV_HEADS * SWA_HD
    gw = SWA_GROUP * SWA_HD
    nkeys = WINDOW + Lq
    nkp = nkeys + 1 + (-(nkeys + 1)) % LANES
    rows_q = SWA_GROUP * Lq

    @pl.when((pl.program_id(0) == 0) & (t == 0))
    def _():
        ri = lax.broadcasted_iota(jnp.int32, (rows_q, nkp), 0)
        ji = lax.broadcasted_iota(jnp.int32, (rows_q, nkp), 1)
        grp = ri // Lq
        dist = jnp.abs((ri - grp * Lq) - ji + WINDOW).astype(F32)
        for kv in range(SWA_KV_HEADS):
            sink = jnp.zeros((rows_q, nkp), F32)
            for gi in range(SWA_GROUP):
                sink = jnp.where(grp == gi, sink_ref[kv * SWA_GROUP + gi] * LOG2E, sink)
            bias = -LOG2E * jnp.exp2(-0.5 * (kv * SWA_GROUP + grp + 1).astype(F32)) * dist
            bias_ref[kv] = jnp.where(ji < nkeys, bias, jnp.where(ji == nkeys, sink, NEG_INF))

    mod = mod_ref[...]
    hb = _modulated_norm(x, ng_ref[...], mod[0:1], mod[1:2]).astype(BF16)
    qkv = _dot(hb, wqkv_ref[...])
    hi = lax.broadcasted_iota(jnp.int32, (gw, gw), 0) // SWA_HD
    hj = lax.broadcasted_iota(jnp.int32, (gw, gw), 1) // SWA_HD
    hsum = jnp.where(hi == hj, 1.0 / SWA_HD, 0.0).astype(BF16)
    q = qkv[:, :nq]
    k = qkv[:, nq:nq + nk]
    q = (q * lax.rsqrt(_head_mean_square(q, hsum) + NORM_EPS) * (qn_ref[...] * (SWA_HD ** -0.5 * LOG2E))).astype(BF16)
    k_new = k * lax.rsqrt(_head_mean_square(k, hsum) + NORM_EPS) * kn_ref[...]
    v_new = qkv[:, nq + nk:]

    if is_prompt:
        @pl.when(t == 0)
        def _():
            kc_ref[...] = jnp.zeros(kc_ref.shape, F32)
            vc_ref[...] = jnp.zeros(vc_ref.shape, F32)
        k_prev = kc_ref[...]
        v_prev = vc_ref[...]
    else:
        k_prev = ck_ref[...]
        v_prev = cv_ref[...]
    k_ext = jnp.concatenate([k_prev, k_new, jnp.zeros((nkp - nkeys, nk), F32)], axis=0)
    v_ext = jnp.concatenate([v_prev, v_new], axis=0)

    low_half = lax.broadcasted_iota(jnp.int32, (Lq, LANES), 1) < SWA_HD
    ji = lax.broadcasted_iota(jnp.int32, (rows_q, nkp), 1)
    is_key = ji < nkeys
    zero = jnp.zeros((), BF16)

    o_cols = []
    for kv in range(SWA_KV_HEADS):
        kt = _tile_head_lanes(k_ext, kv)
        vt = _tile_head_lanes(v_ext, kv)
        bias = bias_ref[kv]
        o_rows = []
        nchunk = tm // Lq
        for c0 in range(0, nchunk, SWA_LOCKSTEP):
            r0s = [c * Lq for c in range(c0, min(c0 + SWA_LOCKSTEP, nchunk))]
            qss = []
            for r0 in r0s:
                qs = []
                for gi in range(SWA_GROUP):
                    qt = q[r0:r0 + Lq, kv * gw + (gi // 2) * LANES:kv * gw + (gi // 2 + 1) * LANES]
                    qs.append(jnp.where(low_half if gi % 2 == 0 else jnp.logical_not(low_half), qt, zero))
                qss.append(jnp.concatenate(qs, axis=0))
            logits = [jnp.where(is_key, _dot_nt(qs, kt[r0:r0 + nkp]), 0.0) + bias for qs, r0 in zip(qss, r0s)]
            if is_prompt:
                logits = [jnp.where(ji >= WINDOW - r0 - t * tm, lg, NEG_INF) if r0 < WINDOW else lg
                          for lg, r0 in zip(logits, r0s)]
            ps = [jnp.exp2(lg - jnp.max(lg, axis=-1, keepdims=True)) for lg in logits]
            dens = [jnp.sum(p, axis=-1, keepdims=True) for p in ps]
            ocs = [_dot(p[:, :nkeys].astype(BF16), vt[r0:r0 + nkeys]) * (1.0 / den)
                   for p, r0, den in zip(ps, r0s, dens)]
            for oc in ocs:
                o_rows.append(jnp.concatenate(
                    [jnp.where(low_half, oc[2 * j * Lq:(2 * j + 1) * Lq], oc[(2 * j + 1) * Lq:(2 * j + 2) * Lq])
                     for j in range(SWA_GROUP // 2)], axis=1))
        o_cols.append(o_rows[0] if len(o_rows) == 1 else jnp.concatenate(o_rows, axis=0))
    o = jnp.concatenate(o_cols, axis=1)
    y_ref[...] = x + mod[2:3] * _dot(o.astype(BF16), wo_ref[...])
    if is_prompt:
        kc_ref[...] = k_new[tm - WINDOW:]
        vc_ref[...] = v_new[tm - WINDOW:]
    else:
        kc_ref[...] = k_new
        vc_ref[...] = v_new


def _swa_layer(x, mod, ng, w, cache, tm, Lq):
    B, T, D = x.shape
    is_prompt = cache is None
    wqkv, qn, kn, sinks, wo = w
    nk = SWA_KV_HEADS * SWA_HD
    in_specs = [
        pl.BlockSpec(memory_space=pltpu.SMEM),
        pl.BlockSpec((None, tm, D), lambda b, t: (b, t, 0)),
        pl.BlockSpec((None, ADA_CHUNKS, D), lambda b, t: (b, 0, 0)),
        _resident((1, D)),
        _resident(wqkv.shape), _resident(qn.shape), _resident(kn.shape), _resident(wo.shape),
    ]
    args = [sinks, x, mod, ng, wqkv, qn, kn, wo]
    if is_prompt:
        rows_out = WINDOW
    else:
        ck, cv = cache
        in_specs += [pl.BlockSpec((None, WINDOW, nk), lambda b, t: (b, 0, 0))] * 2
        args += [ck.reshape(B, WINDOW, nk), cv.reshape(B, WINDOW, nk)]
        rows_out = T
    cache_spec = pl.BlockSpec((None, rows_out, nk), lambda b, t: (b, 0, 0))
    y, kc, vc = pl.pallas_call(
        functools.partial(_swa_kernel, Lq=Lq, is_prompt=is_prompt),
        grid=(B, T // tm),
        in_specs=in_specs,
        out_specs=[pl.BlockSpec((None, tm, D), lambda b, t: (b, t, 0)), cache_spec, cache_spec],
        out_shape=[jax.ShapeDtypeStruct((B, T, D), F32),
                   jax.ShapeDtypeStruct((B, rows_out, nk), F32),
                   jax.ShapeDtypeStruct((B, rows_out, nk), F32)],
        scratch_shapes=[pltpu.VMEM((SWA_KV_HEADS, SWA_GROUP * Lq, WINDOW + Lq + 1 + (-(WINDOW + Lq + 1)) % LANES), F32)],
        compiler_params=_params(("arbitrary", "arbitrary")),
        name="swa_layer",
    )(*args)
    return (y, kc.reshape(B, rows_out, SWA_KV_HEADS, SWA_HD), vc.reshape(B, rows_out, SWA_KV_HEADS, SWA_HD))


def _tile(T, pref):
    return pref if T % pref == 0 else T


def kernel(x_prompt, x_sample, state_gla_l0, state_ret_l1, cache_swa_k_l2, cache_swa_v_l2, state_gla_l3, c_prompt, c_sample, norm_mix, norm_mlp, ada_w, ada_b, mlp_w1, mlp_w2, gla_wq, gla_wk, gla_wv, gla_wg1, gla_wg2, gla_bg, gla_wr, gla_norm, gla_wo, ret_wq, ret_wk, ret_wv, ret_wg, ret_wo, swa_wqkv, swa_qnorm, swa_knorm, swa_sinks, swa_wo):
    bp = x_prompt.shape[0]
    bs = x_sample.shape[0]
    Tp = x_prompt.shape[1]
    Ts = x_sample.shape[1]
    gla_states = [state_gla_l0, state_gla_l3]
    ret_states = [state_ret_l1]
    swa_caches = [(cache_swa_k_l2, cache_swa_v_l2)]

    mod_all = _ada_modulation(jnp.concatenate([c_prompt, c_sample], axis=0), ada_w, ada_b)
    mod_all = mod_all.reshape(DEPTH, bp + bs, ADA_CHUNKS, D_MODEL)

    w1_all = mlp_w1.astype(BF16)
    w2_all = mlp_w2.astype(BF16)
    yp, ys = x_prompt, x_sample
    new_state = []
    for layer in range(DEPTH):
        kind, j = layer % N_MIXERS, layer // N_MIXERS
        mod_p = mod_all[layer, :bp]
        mod_s = mod_all[layer, bp:]
        ng = norm_mix[layer].reshape(1, D_MODEL)
        if kind == 0:
            wqk = jnp.concatenate([gla_wq[j].reshape(D_MODEL, GLA_HEADS, GLA_DK),
                                   gla_wk[j].reshape(D_MODEL, GLA_HEADS, GLA_DK)], axis=2).reshape(D_MODEL, -1)
            w = (wqk.astype(BF16), gla_wv[j].astype(BF16), gla_wg1[j].astype(BF16),
                 gla_wg2[j].astype(BF16), gla_bg[j].reshape(1, -1), gla_wr[j].astype(BF16),
                 gla_norm[j].reshape(1, -1), gla_wo[j].astype(BF16))
            yp, st_p = _gla_layer(yp, mod_p, ng, w, None, _tile(Tp, 1024), min(CHUNK, Tp))
            ys, st_s = _gla_layer(ys, mod_s, ng, w, gla_states[j], Ts, min(CHUNK, Ts))
            new_state += [st_p, st_s]
        elif kind == 1:
            w = (ret_wq[j].astype(BF16), ret_wk[j].astype(BF16), ret_wv[j].astype(BF16), ret_wg[j].astype(BF16),
                 ret_wo[j].astype(BF16))
            yp, st_p = _ret_layer(yp, mod_p, ng, w, None, _tile(Tp, 512))
            ys, st_s = _ret_layer(ys, mod_s, ng, w, ret_states[j], Ts)
            new_state += [st_p, st_s]
        else:
            w = (swa_wqkv[j].astype(BF16), jnp.tile(swa_qnorm[j], SWA_Q_HEADS).reshape(1, -1),
                 jnp.tile(swa_knorm[j], SWA_KV_HEADS).reshape(1, -1), swa_sinks[j], swa_wo[j].astype(BF16))
            yp, k_p, v_p = _swa_layer(yp, mod_p, ng, w, None, _tile(Tp, 1024), CHUNK)
            ys, k_s, v_s = _swa_layer(ys, mod_s, ng, w, swa_caches[j], Ts, Ts)
            new_state += [k_p, v_p, k_s, v_s]
        gm = norm_mlp[layer].reshape(1, D_MODEL)
        yp = _mlp_layer(yp, mod_p[:, :, None, :], gm, w1_all, w2_all, layer, _tile(Tp, 1024))
        mod_rows = jnp.repeat(mod_s.transpose(1, 0, 2), Ts, axis=1)[None]
        ys = _mlp_layer(ys.reshape(1, bs * Ts, D_MODEL), mod_rows, gm, w1_all, w2_all, layer,
                        bs * Ts).reshape(bs, Ts, D_MODEL)
    return (yp, ys, *new_state)
```

```python
import functools
import math

import jax
import jax.numpy as jnp
from jax import lax
from jax.experimental import pallas as pl
from jax.experimental.pallas import tpu as pltpu

F32 = jnp.float32
BF16 = jnp.bfloat16

D_MODEL = 1024
DEPTH = 4
CHUNK = 64
N_MIXERS = 3
GLA_HEADS = 4
GLA_DK = 128
GLA_DV = 256
GLA_RANK = 16
GLA_TAU = 16.0
GLA_RB = 256
GLA_LOCKSTEP = 2
RET_HEADS = 4
RET_DK = 256
RET_DV = 512
RET_RB = 256
RET_LOCKSTEP = 2
SWA_Q_HEADS = 16
SWA_KV_HEADS = 4
SWA_GROUP = 4
SWA_HD = 64
WINDOW = 128
SWA_LOCKSTEP = 4
MLP_HIDDEN = 4 * D_MODEL
ADA_CHUNKS = 6
NORM_EPS = 1e-6
GN_EPS = 1e-5
NEG_INF = -1e30
LOG2E = 1.4426950408889634

VMEM_LIMIT_BYTES = 56 * 1024 * 1024
LANES = 128


def _dot(a, b):
    return jnp.dot(a, b, preferred_element_type=F32)


def _dot_nt(a, b):
    return lax.dot_general(a, b, (((1,), (1,)), ((), ())), preferred_element_type=F32)


def _pad_rows(a, mult):
    r = (-a.shape[0]) % mult
    return a if r == 0 else jnp.concatenate([a, jnp.zeros((r, a.shape[1]), a.dtype)], axis=0)


def _modulated_norm(x, g, shift, scale):
    ms = jnp.mean(x * x, axis=-1, keepdims=True)
    return x * lax.rsqrt(ms + NORM_EPS) * (g * (1.0 + scale)) + shift


def _silu(x):
    return x * (1.0 / (1.0 + jnp.exp2(x * -LOG2E)))


def _resident(shape):
    nd = len(shape)
    return pl.BlockSpec(shape, lambda *_: (0,) * nd, pipeline_mode=pl.Buffered(1))


def _params(sem):
    return pltpu.CompilerParams(dimension_semantics=sem, vmem_limit_bytes=VMEM_LIMIT_BYTES)


def _ada_kernel(c_ref, w_ref, b_ref, o_ref):
    c = c_ref[...]
    a = _silu(c).astype(BF16)
    o_ref[...] = _dot(a, w_ref[...].astype(BF16)) + b_ref[...]


def _ada_modulation(c_all, ada_w, ada_b):
    nb = c_all.shape[0]
    width = ADA_CHUNKS * D_MODEL
    tn = 1536
    return pl.pallas_call(
        _ada_kernel,
        grid=(DEPTH, width // tn),
        in_specs=[
            pl.BlockSpec((nb, D_MODEL), lambda l, j: (0, 0)),
            pl.BlockSpec((None, D_MODEL, tn), lambda l, j: (l, 0, j)),
            pl.BlockSpec((None, 1, tn), lambda l, j: (l, 0, j)),
        ],
        out_specs=pl.BlockSpec((None, nb, tn), lambda l, j: (l, 0, j)),
        out_shape=jax.ShapeDtypeStruct((DEPTH, nb, width), F32),
        compiler_params=_params(("arbitrary", "arbitrary")),
        name="ada_modulation",
    )(c_all, ada_w, ada_b.reshape(DEPTH, 1, width))


def _mlp_kernel(x_ref, mod_ref, g_ref, w1_ref, w2_ref, o_ref, *, th):
    x = x_ref[...]
    h = _modulated_norm(x, g_ref[...], mod_ref[3], mod_ref[4]).astype(BF16)
    acc = jnp.zeros(x.shape, F32)
    for c in range(MLP_HIDDEN // th):
        a = _dot(h, w1_ref[:, c * th:(c + 1) * th])
        a = jnp.maximum(a, 0.0)
        a = a * a
        acc = acc + _dot(a.astype(BF16), w2_ref[c * th:(c + 1) * th, :])
    o_ref[...] = x + mod_ref[5] * acc


def _mlp_layer(x, mod, g, w1_all, w2_all, layer, tm):
    B, T, D = x.shape
    return pl.pallas_call(
        functools.partial(_mlp_kernel, th=512),
        grid=(B, T // tm),
        in_specs=[
            pl.BlockSpec((None, tm, D), lambda b, t: (b, t, 0)),
            pl.BlockSpec((None, ADA_CHUNKS, mod.shape[2], D), lambda b, t: (b, 0, 0, 0)),
            _resident((1, D)),
            pl.BlockSpec((None, D, MLP_HIDDEN), lambda b, t: (layer, 0, 0), pipeline_mode=pl.Buffered(1)),
            pl.BlockSpec((None, MLP_HIDDEN, D), lambda b, t: (layer, 0, 0), pipeline_mode=pl.Buffered(1)),
        ],
        out_specs=pl.BlockSpec((None, tm, D), lambda b, t: (b, t, 0)),
        out_shape=jax.ShapeDtypeStruct((B, T, D), F32),
        compiler_params=_params(("arbitrary", "arbitrary")),
        name="mlp_layer",
    )(x, mod, g, w1_all, w2_all)


def _chunk_cumsum(a, L):
    row = lax.broadcasted_iota(jnp.int32, a.shape, 0) % L
    s = 1
    while s < L:
        a = a + jnp.where(row >= s, pltpu.roll(a, s, axis=0), 0.0)
        s *= 2
    return a


def _gla_kernel(*refs, L, chained):
    if chained:
        (x_ref, mod_ref, ng_ref, wqk_ref, wv_ref, wg1_ref, wg2_ref, bg_ref, wr_ref, gn_ref, wo_ref,
         y_ref, s_ref) = refs

        @pl.when(pl.program_id(1) == 0)
        def _():
            s_ref[...] = jnp.zeros(s_ref.shape, F32)
    else:
        (x_ref, mod_ref, ng_ref, wqk_ref, wv_ref, wg1_ref, wg2_ref, bg_ref, wr_ref, gn_ref, wo_ref,
         s0_ref, y_ref, s_ref) = refs

    x = x_ref[...]
    tm = x.shape[0]
    nc = tm // L
    hb = _modulated_norm(x, ng_ref[...], mod_ref[0], mod_ref[1]).astype(BF16)
    g1 = _dot(hb, wg1_ref[...]).astype(BF16)
    rb = min(tm, GLA_RB)
    ncb = rb // L
    ri = lax.broadcasted_iota(jnp.int32, (rb, rb), 0)
    ci = lax.broadcasted_iota(jnp.int32, (rb, rb), 1)
    mask = ri >= ci
    if ncb > 1:
        mask = mask & (ci >= (ri // L) * L)
    rbp = rb + (-rb) % LANES
    lane_chunk = lax.broadcasted_iota(jnp.int32, (GLA_DK, rbp), 1) // L
    gn = gn_ref[...]

    o_heads = []
    for h0 in range(0, GLA_HEADS, GLA_LOCKSTEP):
        hs = list(range(h0, min(h0 + GLA_LOCKSTEP, GLA_HEADS)))
        kcs = [slice(h * GLA_DK, (h + 1) * GLA_DK) for h in hs]
        vcs = [slice(h * GLA_DV, (h + 1) * GLA_DV) for h in hs]
        zs = [_dot(g1, wg2_ref[:, kc]) + bg_ref[:, kc] for kc in kcs]
        loggs = [(jnp.minimum(z, 0.0) - jnp.log(1.0 + jnp.exp(-jnp.abs(z)))) * (1.0 / GLA_TAU) for z in zs]
        bs = [_chunk_cumsum(lg, L) for lg in loggs]
        ebs = [jnp.exp(b) for b in bs]
        enbs = [jnp.exp(-b) for b in bs]
        ebls = [[jnp.exp(b[c * L + L - 1:c * L + L, :]) for c in range(nc)] for b in bs]
        ebl_rows = [[jnp.broadcast_to(e, (L, GLA_DK)) for e in ebl] for ebl in ebls]
        ebl_rows = [r[0] if nc == 1 else jnp.concatenate(r, axis=0) for r in ebl_rows]
        qks = [_dot(hb, wqk_ref[:, 2 * h * GLA_DK:2 * (h + 1) * GLA_DK]) for h in hs]
        qds = [(qk[:, :GLA_DK] * (eb * GLA_DK ** -0.5)).astype(BF16) for qk, eb in zip(qks, ebs)]
        kdfs = [qk[:, GLA_DK:] * enb for qk, enb in zip(qks, enbs)]
        kds = [kdf.astype(BF16) for kdf in kdfs]
        kdecs = [kdf * r for kdf, r in zip(kdfs, ebl_rows)]
        vs = [_dot(hb, wv_ref[:, vc]).astype(BF16) for vc in vcs]
        gates = [_silu(_dot(hb, wr_ref[:, vc])) for vc in vcs]
        Ss = [s_ref[h] for h in hs] if chained else None
        o_blocks = [[] for _ in hs]
        for r0 in range(0, tm, rb):
            rows = slice(r0, r0 + rb)
            vhs = [_pad_rows(v[rows], LANES) for v in vs]
            atts = [jnp.where(mask, _dot_nt(qd[rows], kd[rows]), 0.0).astype(BF16) for qd, kd in zip(qds, kds)]
            os_ = [_dot(att, vh[:rb]) for att, vh in zip(atts, vhs)]
            kdec_ts = [_pad_rows(kdec[rows], LANES).T for kdec in kdecs]
            lhss = [[jnp.where(lane_chunk == c, kdec_t, 0.0) for c in range(ncb)] for kdec_t in kdec_ts]
            lhss = [(l[0] if ncb == 1 else jnp.concatenate(l, axis=0)).astype(BF16) for l in lhss]
            us = [_dot(lhs, vh) for lhs, vh in zip(lhss, vhs)]
            o_inters = [[] for _ in hs]
            for c in range(ncb):
                if not chained:
                    Ss = [s0_ref[r0 // L + c, h] for h in hs]
                for i in range(len(hs)):
                    o_inters[i].append(_dot(qds[i][r0 + c * L:r0 + (c + 1) * L], Ss[i].astype(BF16)))
                dcols = [jnp.broadcast_to(ebl[r0 // L + c], (GLA_DK, GLA_DK)).T for ebl in ebls]
                Ss = [S * jnp.concatenate([dcol, dcol], axis=1) + u[c * GLA_DK:(c + 1) * GLA_DK]
                      for S, dcol, u in zip(Ss, dcols, us)]
                if not chained:
                    for i, h in enumerate(hs):
                        s_ref[r0 // L + c, h] = Ss[i]
            for i in range(len(hs)):
                oi = o_inters[i]
                o_blocks[i].append(os_[i] + (oi[0] if ncb == 1 else jnp.concatenate(oi, axis=0)))
        for i, h in enumerate(hs):
            if chained:
                s_ref[h] = Ss[i]
            o = o_blocks[i][0] if len(o_blocks[i]) == 1 else jnp.concatenate(o_blocks[i], axis=0)
            o = o * lax.rsqrt(jnp.mean(o * o, axis=-1, keepdims=True) + NORM_EPS) * gn
            o_heads.append((o * gates[i]).astype(BF16))
    y_ref[...] = x + mod_ref[2] * _dot(jnp.concatenate(o_heads, axis=1), wo_ref[...])


def _gla_layer(x, mod, ng, w, s0, tm, L):
    B, T, D = x.shape
    chained = s0 is None
    weights = [ng] + list(w)
    in_specs = [
        pl.BlockSpec((None, tm, D), lambda b, t: (b, t, 0)),
        pl.BlockSpec((None, ADA_CHUNKS, mod.shape[2], D), lambda b, t: (b, 0, 0, 0)),
    ] + [_resident(a.shape) for a in weights]
    args = [x, mod] + weights
    if chained:
        state_shape = (B, GLA_HEADS, GLA_DK, GLA_DV)
        state_spec = pl.BlockSpec((None,) + state_shape[1:], lambda b, t: (b, 0, 0, 0))
    else:
        assert B == 1 and tm == T and s0.shape[0] * L == T
        state_shape = s0.shape
        state_spec = pl.BlockSpec(state_shape, lambda b, t: (0, 0, 0, 0))
        in_specs.append(state_spec)
        args.append(s0)
    return pl.pallas_call(
        functools.partial(_gla_kernel, L=L, chained=chained),
        grid=(B, T // tm),
        in_specs=in_specs,
        out_specs=[pl.BlockSpec((None, tm, D), lambda b, t: (b, t, 0)), state_spec],
        out_shape=[jax.ShapeDtypeStruct((B, T, D), F32), jax.ShapeDtypeStruct(state_shape, F32)],
        compiler_params=_params(("arbitrary", "arbitrary")),
        name="gla_layer",
    )(*args)


def _ret_kernel(*refs, has_state):
    if has_state:
        (x_ref, mod_ref, ng_ref, wq_ref, wk_ref, wv_ref, wg_ref, wo_ref, s0_ref, y_ref, s_ref,
         dintra_ref, dquery_ref, dkey_ref) = refs
    else:
        (x_ref, mod_ref, ng_ref, wq_ref, wk_ref, wv_ref, wg_ref, wo_ref, y_ref, s_ref,
         dintra_ref, dquery_ref, dkey_ref) = refs
    tm = x_ref.shape[0]
    L = min(tm, RET_RB)
    log_decay = [math.log(1.0 - 2.0 ** (-5.0 - h)) for h in range(RET_HEADS)]

    @pl.when((pl.program_id(0) == 0) & (pl.program_id(1) == 0))
    def _():
        rel = (lax.broadcasted_iota(jnp.int32, (L, L), 0) - lax.broadcasted_iota(jnp.int32, (L, L), 1)).astype(F32)
        pos = lax.broadcasted_iota(jnp.int32, (L, LANES), 0).astype(F32)
        for h in range(RET_HEADS):
            dintra_ref[h] = jnp.where(rel >= 0.0, jnp.exp(log_decay[h] * jnp.maximum(rel, 0.0)), 0.0)
            dquery_ref[h] = jnp.exp(log_decay[h] * (pos + 1.0))
            dkey_ref[h] = jnp.exp(log_decay[h] * (L - 1.0 - pos))

    @pl.when(pl.program_id(1) == 0)
    def _():
        if has_state:
            s_ref[...] = s0_ref[...]
        else:
            s_ref[...] = jnp.zeros(s_ref.shape, F32)

    x = x_ref[...]
    mod = mod_ref[...]
    hb = _modulated_norm(x, ng_ref[...], mod[0:1], mod[1:2]).astype(BF16)
    q = _dot(hb, wq_ref[...])
    k = _dot(hb, wk_ref[...]) * (RET_DK ** -0.5)
    v = _dot(hb, wv_ref[...])
    g = _dot(hb, wg_ref[...])
    qb = q.astype(BF16)
    kb = k.astype(BF16)
    vb = _pad_rows(v.astype(BF16), LANES)

    o_heads = []
    for h0 in range(0, RET_HEADS, RET_LOCKSTEP):
        hs = list(range(h0, min(h0 + RET_LOCKSTEP, RET_HEADS)))
        kcs = [slice(h * RET_DK, (h + 1) * RET_DK) for h in hs]
        Ss = [s_ref[h] for h in hs]
        o_blocks = [[] for _ in hs]
        for r0 in range(0, tm, L):
            rows = slice(r0, r0 + L)
            vhs = [vb[r0:r0 + vb.shape[0] - tm + L, h * RET_DV:(h + 1) * RET_DV] for h in hs]
            atts = [(_dot_nt(qb[rows, kc], kb[rows, kc]) * dintra_ref[h]).astype(BF16) for h, kc in zip(hs, kcs)]
            os_ = [_dot(att, vh[:L]) + _dot(qb[rows, kc], S.astype(BF16))
                   * jnp.concatenate([dquery_ref[h]] * (RET_DV // LANES), axis=1)
                   for att, vh, kc, S, h in zip(atts, vhs, kcs, Ss, hs)]
            k_decs = [_pad_rows(k[rows, kc] * jnp.concatenate([dkey_ref[h]] * (RET_DK // LANES), axis=1), LANES)
                      for h, kc in zip(hs, kcs)]
            Ss = [S * math.exp(log_decay[h] * L) + _dot(k_dec.T.astype(BF16), vh)
                  for h, S, k_dec, vh in zip(hs, Ss, k_decs, vhs)]
            ocs = [o - jnp.mean(o, axis=-1, keepdims=True) for o in os_]
            for i, oc in enumerate(ocs):
                o_blocks[i].append(oc * lax.rsqrt(jnp.mean(oc * oc, axis=-1, keepdims=True) + GN_EPS))
        for i, h in enumerate(hs):
            s_ref[h] = Ss[i]
            o_heads.append(o_blocks[i][0] if len(o_blocks[i]) == 1 else jnp.concatenate(o_blocks[i], axis=0))
    o = jnp.concatenate(o_heads, axis=1)
    o = (o * _silu(g)).astype(BF16)
    y_ref[...] = x + mod[2:3] * _dot(o, wo_ref[...])


def _ret_layer(x, mod, ng, w, s0, tm):
    B, T, D = x.shape
    has_state = s0 is not None
    wq, wk, wv, wg, wo = w
    state_spec = pl.BlockSpec((None, RET_HEADS, RET_DK, RET_DV), lambda b, t: (b, 0, 0, 0))
    in_specs = [
        pl.BlockSpec((None, tm, D), lambda b, t: (b, t, 0)),
        pl.BlockSpec((None, ADA_CHUNKS, D), lambda b, t: (b, 0, 0)),
        _resident((1, D)),
        _resident(wq.shape), _resident(wk.shape), _resident(wv.shape), _resident(wg.shape), _resident(wo.shape),
    ]
    args = [x, mod, ng, wq, wk, wv, wg, wo]
    if has_state:
        in_specs.append(state_spec)
        args.append(s0)
    return pl.pallas_call(
        functools.partial(_ret_kernel, has_state=has_state),
        grid=(B, T // tm),
        in_specs=in_specs,
        out_specs=[pl.BlockSpec((None, tm, D), lambda b, t: (b, t, 0)), state_spec],
        out_shape=[jax.ShapeDtypeStruct((B, T, D), F32),
                   jax.ShapeDtypeStruct((B, RET_HEADS, RET_DK, RET_DV), F32)],
        scratch_shapes=[pltpu.VMEM((RET_HEADS, min(tm, RET_RB), min(tm, RET_RB)), F32),
                        pltpu.VMEM((RET_HEADS, min(tm, RET_RB), LANES), F32),
                        pltpu.VMEM((RET_HEADS, min(tm, RET_RB), LANES), F32)],
        compiler_params=_params(("arbitrary", "arbitrary")),
        name="ret_layer",
    )(*args)


def _head_mean_square(a, hsum):
    w = hsum.shape[0]
    parts = [_dot((a[:, j:j + w] * a[:, j:j + w]).astype(BF16), hsum) for j in range(0, a.shape[1], w)]
    return parts[0] if len(parts) == 1 else jnp.concatenate(parts, axis=1)


def _tile_head_lanes(a, kv):
    t = a[:, (kv // 2) * LANES:(kv // 2 + 1) * LANES]
    low = lax.broadcasted_iota(jnp.int32, t.shape, 1) < SWA_HD
    keep = low if kv % 2 == 0 else jnp.logical_not(low)
    return jnp.where(keep, t, pltpu.roll(t, SWA_HD, axis=1)).astype(BF16)


def _swa_kernel(*refs, Lq, is_prompt):
    if is_prompt:
        (sink_ref, x_ref, mod_ref, ng_ref, wqkv_ref, qn_ref, kn_ref, wo_ref, y_ref, kc_ref, vc_ref,
         bias_ref) = refs
    else:
        (sink_ref, x_ref, mod_ref, ng_ref, wqkv_ref, qn_ref, kn_ref, wo_ref, ck_ref, cv_ref,
         y_ref, kc_ref, vc_ref, bias_ref) = refs
    t = pl.program_id(1)
    x = x_ref[...]
    tm = x.shape[0]
    nq = SWA_Q_HEADS * SWA_HD
    nk = SWA_KV_HEADS * SWA_HD
    gw = SWA_GROUP * SWA_HD
    nkeys = WINDOW + Lq
    nkp = nkeys + 1 + (-(nkeys + 1)) % LANES
    rows_q = SWA_GROUP * Lq

    @pl.when((pl.program_id(0) == 0) & (t == 0))
    def _():
        ri = lax.broadcasted_iota(jnp.int32, (rows_q, nkp), 0)
        ji = lax.broadcasted_iota(jnp.int32, (rows_q, nkp), 1)
        grp = ri // Lq
        dist = jnp.abs((ri - grp * Lq) - ji + WINDOW).astype(F32)
        for kv in range(SWA_KV_HEADS):
            sink = jnp.zeros((rows_q, nkp), F32)
            for gi in range(SWA_GROUP):
                sink = jnp.where(grp == gi, sink_ref[kv * SWA_GROUP + gi] * LOG2E, sink)
            bias = -LOG2E * jnp.exp2(-0.5 * (kv * SWA_GROUP + grp + 1).astype(F32)) * dist
            bias_ref[kv] = jnp.where(ji < nkeys, bias, jnp.where(ji == nkeys, sink, NEG_INF))

    mod = mod_ref[...]
    hb = _modulated_norm(x, ng_ref[...], mod[0:1], mod[1:2]).astype(BF16)
    hi = lax.broadcasted_iota(jnp.int32, (gw, gw), 0) // SWA_HD
    hj = lax.broadcasted_iota(jnp.int32, (gw, gw), 1) // SWA_HD
    hsum = jnp.where(hi == hj, 1.0 / SWA_HD, 0.0).astype(BF16)
    kv_new = _dot(hb, wqkv_ref[:, nq:])
    k = kv_new[:, :nk]
    v_new = kv_new[:, nk:]
    k_new = k * lax.rsqrt(_head_mean_square(k, hsum) + NORM_EPS) * kn_ref[...]
    q_gain = qn_ref[...] * (SWA_HD ** -0.5 * LOG2E)
    qs_kv = []
    for kv0 in range(0, SWA_KV_HEADS, 2):
        qg = _dot(hb, wqkv_ref[:, kv0 * gw:(kv0 + 2) * gw])
        qg = (qg * lax.rsqrt(_head_mean_square(qg, hsum) + NORM_EPS) * q_gain[:, kv0 * gw:(kv0 + 2) * gw]).astype(BF16)
        qs_kv += [qg[:, :gw], qg[:, gw:]]

    if is_prompt:
        @pl.when(t == 0)
        def _():
            kc_ref[...] = jnp.zeros(kc_ref.shape, F32)
            vc_ref[...] = jnp.zeros(vc_ref.shape, F32)
        k_prev = kc_ref[...]
        v_prev = vc_ref[...]
    else:
        k_prev = ck_ref[...]
        v_prev = cv_ref[...]
    k_ext = jnp.concatenate([k_prev, k_new, jnp.zeros((nkp - nkeys, nk), F32)], axis=0)
    v_ext = jnp.concatenate([v_prev, v_new], axis=0)

    low_half = lax.broadcasted_iota(jnp.int32, (Lq, LANES), 1) < SWA_HD
    ji = lax.broadcasted_iota(jnp.int32, (rows_q, nkp), 1)
    is_key = ji < nkeys
    zero = jnp.zeros((), BF16)

    o_cols = []
    for kv in range(SWA_KV_HEADS):
        kt = _tile_head_lanes(k_ext, kv)
        vt = _tile_head_lanes(v_ext, kv)
        bias = bias_ref[kv]
        o_rows = []
        nchunk = tm // Lq
        for c0 in range(0, nchunk, SWA_LOCKSTEP):
            r0s = [c * Lq for c in range(c0, min(c0 + SWA_LOCKSTEP, nchunk))]
            qss = []
            for r0 in r0s:
                qs = []
                for gi in range(SWA_GROUP):
                    qt = qs_kv[kv][r0:r0 + Lq, (gi // 2) * LANES:(gi // 2 + 1) * LANES]
                    qs.append(jnp.where(low_half if gi % 2 == 0 else jnp.logical_not(low_half), qt, zero))
                qss.append(jnp.concatenate(qs, axis=0))
            logits = [jnp.where(is_key, _dot_nt(qs, kt[r0:r0 + nkp]), 0.0) + bias for qs, r0 in zip(qss, r0s)]
            if is_prompt:
                logits = [jnp.where(ji >= WINDOW - r0 - t * tm, lg, NEG_INF) if r0 < WINDOW else lg
                          for lg, r0 in zip(logits, r0s)]
            ps = [jnp.exp2(lg - jnp.max(lg, axis=-1, keepdims=True)) for lg in logits]
            dens = [jnp.sum(p, axis=-1, keepdims=True) for p in ps]
            ocs = [_dot(p[:, :nkeys].astype(BF16), vt[r0:r0 + nkeys]) * (1.0 / den)
                   for p, r0, den in zip(ps, r0s, dens)]
            for oc in ocs:
                o_rows.append(jnp.concatenate(
                    [jnp.where(low_half, oc[2 * j * Lq:(2 * j + 1) * Lq], oc[(2 * j + 1) * Lq:(2 * j + 2) * Lq])
                     for j in range(SWA_GROUP // 2)], axis=1))
        o_cols.append(o_rows[0] if len(o_rows) == 1 else jnp.concatenate(o_rows, axis=0))
    o = jnp.concatenate(o_cols, axis=1)
    y_ref[...] = x + mod[2:3] * _dot(o.astype(BF16), wo_ref[...])
    if is_prompt:
        kc_ref[...] = k_new[tm - WINDOW:]
        vc_ref[...] = v_new[tm - WINDOW:]
    else:
        kc_ref[...] = k_new
        vc_ref[...] = v_new


def _swa_layer(x, mod, ng, w, cache, tm, Lq):
    B, T, D = x.shape
    is_prompt = cache is None
    wqkv, qn, kn, sinks, wo = w
    nk = SWA_KV_HEADS * SWA_HD
    in_specs = [
        pl.BlockSpec(memory_space=pltpu.SMEM),
        pl.BlockSpec((None, tm, D), lambda b, t: (b, t, 0)),
        pl.BlockSpec((None, ADA_CHUNKS, D), lambda b, t: (b, 0, 0)),
        _resident((1, D)),
        _resident(wqkv.shape), _resident(qn.shape), _resident(kn.shape), _resident(wo.shape),
    ]
    args = [sinks, x, mod, ng, wqkv, qn, kn, wo]
    if is_prompt:
        rows_out = WINDOW
    else:
        ck, cv = cache
        in_specs += [pl.BlockSpec((None, WINDOW, nk), lambda b, t: (b, 0, 0))] * 2
        args += [ck.reshape(B, WINDOW, nk), cv.reshape(B, WINDOW, nk)]
        rows_out = T
    cache_spec = pl.BlockSpec((None, rows_out, nk), lambda b, t: (b, 0, 0))
    y, kc, vc = pl.pallas_call(
        functools.partial(_swa_kernel, Lq=Lq, is_prompt=is_prompt),
        grid=(B, T // tm),
        in_specs=in_specs,
        out_specs=[pl.BlockSpec((None, tm, D), lambda b, t: (b, t, 0)), cache_spec, cache_spec],
        out_shape=[jax.ShapeDtypeStruct((B, T, D), F32),
                   jax.ShapeDtypeStruct((B, rows_out, nk), F32),
                   jax.ShapeDtypeStruct((B, rows_out, nk), F32)],
        scratch_shapes=[pltpu.VMEM((SWA_KV_HEADS, SWA_GROUP * Lq, WINDOW + Lq + 1 + (-(WINDOW + Lq + 1)) % LANES), F32)],
        compiler_params=_params(("arbitrary", "arbitrary")),
        name="swa_layer",
    )(*args)
    return (y, kc.reshape(B, rows_out, SWA_KV_HEADS, SWA_HD), vc.reshape(B, rows_out, SWA_KV_HEADS, SWA_HD))


def _tile(T, pref):
    return pref if T % pref == 0 else T


def kernel(x_prompt, x_sample, state_gla_l0, state_ret_l1, cache_swa_k_l2, cache_swa_v_l2, state_gla_l3, c_prompt, c_sample, norm_mix, norm_mlp, ada_w, ada_b, mlp_w1, mlp_w2, gla_wq, gla_wk, gla_wv, gla_wg1, gla_wg2, gla_bg, gla_wr, gla_norm, gla_wo, ret_wq, ret_wk, ret_wv, ret_wg, ret_wo, swa_wqkv, swa_qnorm, swa_knorm, swa_sinks, swa_wo):
    bp = x_prompt.shape[0]
    bs = x_sample.shape[0]
    Tp = x_prompt.shape[1]
    Ts = x_sample.shape[1]
    gla_states = [state_gla_l0, state_gla_l3]
    ret_states = [state_ret_l1]
    swa_caches = [(cache_swa_k_l2, cache_swa_v_l2)]

    mod_all = _ada_modulation(jnp.concatenate([c_prompt, c_sample], axis=0), ada_w, ada_b)
    mod_all = mod_all.reshape(DEPTH, bp + bs, ADA_CHUNKS, D_MODEL)

    w1_all = mlp_w1.astype(BF16)
    w2_all = mlp_w2.astype(BF16)
    yp, ys = x_prompt, x_sample
    new_state = []
    for layer in range(DEPTH):
        kind, j = layer % N_MIXERS, layer // N_MIXERS
        mod_p = mod_all[layer, :bp]
        mod_s = mod_all[layer, bp:]
        mod_rows = jnp.repeat(mod_s.transpose(1, 0, 2), Ts, axis=1)[None]
        ng = norm_mix[layer].reshape(1, D_MODEL)
        if kind == 0:
            wqk = jnp.concatenate([gla_wq[j].reshape(D_MODEL, GLA_HEADS, GLA_DK),
                                   gla_wk[j].reshape(D_MODEL, GLA_HEADS, GLA_DK)], axis=2).reshape(D_MODEL, -1)
            w = (wqk.astype(BF16), gla_wv[j].astype(BF16), gla_wg1[j].astype(BF16),
                 gla_wg2[j].astype(BF16), gla_bg[j].reshape(1, -1), gla_wr[j].astype(BF16),
                 gla_norm[j].reshape(1, -1), gla_wo[j].astype(BF16))
            yp, st_p = _gla_layer(yp, mod_p[:, :, None, :], ng, w, None, _tile(Tp, 1024), min(CHUNK, Tp))
            assert Ts <= CHUNK
            ys, st_s = _gla_layer(ys.reshape(1, bs * Ts, D_MODEL), mod_rows, ng, w, gla_states[j], bs * Ts, Ts)
            ys = ys.reshape(bs, Ts, D_MODEL)
            new_state += [st_p, st_s]
        elif kind == 1:
            w = (ret_wq[j].astype(BF16), ret_wk[j].astype(BF16), ret_wv[j].astype(BF16), ret_wg[j].astype(BF16),
                 ret_wo[j].astype(BF16))
            yp, st_p = _ret_layer(yp, mod_p, ng, w, None, _tile(Tp, 512))
            ys, st_s = _ret_layer(ys, mod_s, ng, w, ret_states[j], Ts)
            new_state += [st_p, st_s]
        else:
            w = (swa_wqkv[j].astype(BF16), jnp.tile(swa_qnorm[j], SWA_Q_HEADS).reshape(1, -1),
                 jnp.tile(swa_knorm[j], SWA_KV_HEADS).reshape(1, -1), swa_sinks[j], swa_wo[j].astype(BF16))
            yp, k_p, v_p = _swa_layer(yp, mod_p, ng, w, None, _tile(Tp, 1024), CHUNK)
            ys, k_s, v_s = _swa_layer(ys, mod_s, ng, w, swa_caches[j], Ts, Ts)
            new_state += [k_p, v_p, k_s, v_s]
        gm = norm_mlp[layer].reshape(1, D_MODEL)
        yp = _mlp_layer(yp, mod_p[:, :, None, :], gm, w1_all, w2_all, layer, _tile(Tp, 1024))
        ys = _mlp_layer(ys.reshape(1, bs * Ts, D_MODEL), mod_rows, gm, w1_all, w2_all, layer,
                        bs * Ts).reshape(bs, Ts, D_MODEL)
    return (yp, ys, *new_state)
```

```python
import functools
import math

import jax
import jax.numpy as jnp
from jax import lax
from jax.experimental import pallas as pl
from jax.experimental.pallas import tpu as pltpu

F32 = jnp.float32
BF16 = jnp.bfloat16

D_MODEL = 1024
DEPTH = 4
CHUNK = 64
N_MIXERS = 3
GLA_HEADS = 4
GLA_DK = 128
GLA_DV = 256
GLA_RANK = 16
GLA_TAU = 16.0
GLA_RB = 256
GLA_LOCKSTEP = 2
RET_HEADS = 4
RET_DK = 256
RET_DV = 512
RET_RB = 256
RET_LOCKSTEP = 2
RET_SAMPLE_GROUP = 4
SWA_Q_HEADS = 16
SWA_KV_HEADS = 4
SWA_GROUP = 4
SWA_HD = 64
WINDOW = 128
SWA_LOCKSTEP = 4
MLP_HIDDEN = 4 * D_MODEL
ADA_CHUNKS = 6
NORM_EPS = 1e-6
GN_EPS = 1e-5
NEG_INF = -1e30
LOG2E = 1.4426950408889634

VMEM_LIMIT_BYTES = 56 * 1024 * 1024
LANES = 128


def _dot(a, b):
    return jnp.dot(a, b, preferred_element_type=F32)


def _dot_nt(a, b):
    return lax.dot_general(a, b, (((1,), (1,)), ((), ())), preferred_element_type=F32)


def _pad_rows(a, mult):
    r = (-a.shape[0]) % mult
    return a if r == 0 else jnp.concatenate([a, jnp.zeros((r, a.shape[1]), a.dtype)], axis=0)


def _modulated_norm(x, g, shift, scale):
    ms = jnp.mean(x * x, axis=-1, keepdims=True)
    return x * lax.rsqrt(ms + NORM_EPS) * (g * (1.0 + scale)) + shift


def _silu(x):
    return x * (1.0 / (1.0 + jnp.exp2(x * -LOG2E)))


def _resident(shape):
    nd = len(shape)
    return pl.BlockSpec(shape, lambda *_: (0,) * nd, pipeline_mode=pl.Buffered(1))


def _params(sem):
    return pltpu.CompilerParams(dimension_semantics=sem, vmem_limit_bytes=VMEM_LIMIT_BYTES)


def _ada_kernel(c_ref, w_ref, b_ref, o_ref):
    c = c_ref[...]
    a = _silu(c).astype(BF16)
    o_ref[...] = _dot(a, w_ref[...].astype(BF16)) + b_ref[...]


def _ada_modulation(c_all, ada_w, ada_b):
    nb = c_all.shape[0]
    width = ADA_CHUNKS * D_MODEL
    tn = 1536
    return pl.pallas_call(
        _ada_kernel,
        grid=(DEPTH, width // tn),
        in_specs=[
            pl.BlockSpec((nb, D_MODEL), lambda l, j: (0, 0)),
            pl.BlockSpec((None, D_MODEL, tn), lambda l, j: (l, 0, j)),
            pl.BlockSpec((None, 1, tn), lambda l, j: (l, 0, j)),
        ],
        out_specs=pl.BlockSpec((None, nb, tn), lambda l, j: (l, 0, j)),
        out_shape=jax.ShapeDtypeStruct((DEPTH, nb, width), F32),
        compiler_params=_params(("arbitrary", "arbitrary")),
        name="ada_modulation",
    )(c_all, ada_w, ada_b.reshape(DEPTH, 1, width))


def _mlp_kernel(x_ref, mod_ref, g_ref, w1_ref, w2_ref, o_ref, *, th):
    x = x_ref[...]
    h = _modulated_norm(x, g_ref[...], mod_ref[3], mod_ref[4]).astype(BF16)
    acc = jnp.zeros(x.shape, F32)
    for c in range(MLP_HIDDEN // th):
        a = _dot(h, w1_ref[:, c * th:(c + 1) * th])
        a = jnp.maximum(a, 0.0)
        a = a * a
        acc = acc + _dot(a.astype(BF16), w2_ref[c * th:(c + 1) * th, :])
    o_ref[...] = x + mod_ref[5] * acc


def _mlp_layer(x, mod, g, w1_all, w2_all, layer, tm):
    B, T, D = x.shape
    return pl.pallas_call(
        functools.partial(_mlp_kernel, th=512),
        grid=(B, T // tm),
        in_specs=[
            pl.BlockSpec((None, tm, D), lambda b, t: (b, t, 0)),
            pl.BlockSpec((None, ADA_CHUNKS, mod.shape[2], D), lambda b, t: (b, 0, 0, 0)),
            _resident((1, D)),
            pl.BlockSpec((None, D, MLP_HIDDEN), lambda b, t: (layer, 0, 0), pipeline_mode=pl.Buffered(1)),
            pl.BlockSpec((None, MLP_HIDDEN, D), lambda b, t: (layer, 0, 0), pipeline_mode=pl.Buffered(1)),
        ],
        out_specs=pl.BlockSpec((None, tm, D), lambda b, t: (b, t, 0)),
        out_shape=jax.ShapeDtypeStruct((B, T, D), F32),
        compiler_params=_params(("arbitrary", "arbitrary")),
        name="mlp_layer",
    )(x, mod, g, w1_all, w2_all)


def _chunk_cumsum(a, L):
    row = lax.broadcasted_iota(jnp.int32, a.shape, 0) % L
    s = 1
    while s < L:
        a = a + jnp.where(row >= s, pltpu.roll(a, s, axis=0), 0.0)
        s *= 2
    return a


def _gla_kernel(*refs, L, chained):
    if chained:
        (x_ref, mod_ref, ng_ref, wqk_ref, wv_ref, wg1_ref, wg2_ref, bg_ref, wr_ref, gn_ref, wo_ref,
         y_ref, s_ref) = refs

        @pl.when(pl.program_id(1) == 0)
        def _():
            s_ref[...] = jnp.zeros(s_ref.shape, F32)
    else:
        (x_ref, mod_ref, ng_ref, wqk_ref, wv_ref, wg1_ref, wg2_ref, bg_ref, wr_ref, gn_ref, wo_ref,
         s0_ref, y_ref, s_ref) = refs

    x = x_ref[...]
    tm = x.shape[0]
    nc = tm // L
    hb = _modulated_norm(x, ng_ref[...], mod_ref[0], mod_ref[1]).astype(BF16)
    g1 = _dot(hb, wg1_ref[...]).astype(BF16)
    rb = min(tm, GLA_RB)
    ncb = rb // L
    ri = lax.broadcasted_iota(jnp.int32, (rb, rb), 0)
    ci = lax.broadcasted_iota(jnp.int32, (rb, rb), 1)
    mask = ri >= ci
    if ncb > 1:
        mask = mask & (ci >= (ri // L) * L)
    rbp = rb + (-rb) % LANES
    lane_chunk = lax.broadcasted_iota(jnp.int32, (GLA_DK, rbp), 1) // L
    gn = gn_ref[...]

    o_heads = []
    for h0 in range(0, GLA_HEADS, GLA_LOCKSTEP):
        hs = list(range(h0, min(h0 + GLA_LOCKSTEP, GLA_HEADS)))
        kcs = [slice(h * GLA_DK, (h + 1) * GLA_DK) for h in hs]
        vcs = [slice(h * GLA_DV, (h + 1) * GLA_DV) for h in hs]
        zs = [_dot(g1, wg2_ref[:, kc]) + bg_ref[:, kc] for kc in kcs]
        loggs = [(jnp.minimum(z, 0.0) - jnp.log(1.0 + jnp.exp(-jnp.abs(z)))) * (1.0 / GLA_TAU) for z in zs]
        bs = [_chunk_cumsum(lg, L) for lg in loggs]
        ebs = [jnp.exp(b) for b in bs]
        enbs = [jnp.exp(-b) for b in bs]
        ebls = [[jnp.exp(b[c * L + L - 1:c * L + L, :]) for c in range(nc)] for b in bs]
        ebl_rows = [[jnp.broadcast_to(e, (L, GLA_DK)) for e in ebl] for ebl in ebls]
        ebl_rows = [r[0] if nc == 1 else jnp.concatenate(r, axis=0) for r in ebl_rows]
        qks = [_dot(hb, wqk_ref[:, 2 * h * GLA_DK:2 * (h + 1) * GLA_DK]) for h in hs]
        qds = [(qk[:, :GLA_DK] * (eb * GLA_DK ** -0.5)).astype(BF16) for qk, eb in zip(qks, ebs)]
        kdfs = [qk[:, GLA_DK:] * enb for qk, enb in zip(qks, enbs)]
        kds = [kdf.astype(BF16) for kdf in kdfs]
        kdecs = [kdf * r for kdf, r in zip(kdfs, ebl_rows)]
        vs = [_dot(hb, wv_ref[:, vc]).astype(BF16) for vc in vcs]
        gates = [_silu(_dot(hb, wr_ref[:, vc])) for vc in vcs]
        Ss = [s_ref[h] for h in hs] if chained else None
        o_blocks = [[] for _ in hs]
        for r0 in range(0, tm, rb):
            rows = slice(r0, r0 + rb)
            vhs = [_pad_rows(v[rows], LANES) for v in vs]
            atts = [jnp.where(mask, _dot_nt(qd[rows], kd[rows]), 0.0).astype(BF16) for qd, kd in zip(qds, kds)]
            os_ = [_dot(att, vh[:rb]) for att, vh in zip(atts, vhs)]
            kdec_ts = [_pad_rows(kdec[rows], LANES).T for kdec in kdecs]
            lhss = [[jnp.where(lane_chunk == c, kdec_t, 0.0) for c in range(ncb)] for kdec_t in kdec_ts]
            lhss = [(l[0] if ncb == 1 else jnp.concatenate(l, axis=0)).astype(BF16) for l in lhss]
            us = [_dot(lhs, vh) for lhs, vh in zip(lhss, vhs)]
            o_inters = [[] for _ in hs]
            for c in range(ncb):
                if not chained:
                    Ss = [s0_ref[r0 // L + c, h] for h in hs]
                for i in range(len(hs)):
                    o_inters[i].append(_dot(qds[i][r0 + c * L:r0 + (c + 1) * L], Ss[i].astype(BF16)))
                dcols = [jnp.broadcast_to(ebl[r0 // L + c], (GLA_DK, GLA_DK)).T for ebl in ebls]
                Ss = [S * jnp.concatenate([dcol, dcol], axis=1) + u[c * GLA_DK:(c + 1) * GLA_DK]
                      for S, dcol, u in zip(Ss, dcols, us)]
                if not chained:
                    for i, h in enumerate(hs):
                        s_ref[r0 // L + c, h] = Ss[i]
            for i in range(len(hs)):
                oi = o_inters[i]
                o_blocks[i].append(os_[i] + (oi[0] if ncb == 1 else jnp.concatenate(oi, axis=0)))
        for i, h in enumerate(hs):
            if chained:
                s_ref[h] = Ss[i]
            o = o_blocks[i][0] if len(o_blocks[i]) == 1 else jnp.concatenate(o_blocks[i], axis=0)
            o = o * lax.rsqrt(jnp.mean(o * o, axis=-1, keepdims=True) + NORM_EPS) * gn
            o_heads.append((o * gates[i]).astype(BF16))
    y_ref[...] = x + mod_ref[2] * _dot(jnp.concatenate(o_heads, axis=1), wo_ref[...])


def _gla_layer(x, mod, ng, w, s0, tm, L):
    B, T, D = x.shape
    chained = s0 is None
    weights = [ng] + list(w)
    in_specs = [
        pl.BlockSpec((None, tm, D), lambda b, t: (b, t, 0)),
        pl.BlockSpec((None, ADA_CHUNKS, mod.shape[2], D), lambda b, t: (b, 0, 0, 0)),
    ] + [_resident(a.shape) for a in weights]
    args = [x, mod] + weights
    if chained:
        state_shape = (B, GLA_HEADS, GLA_DK, GLA_DV)
        state_spec = pl.BlockSpec((None,) + state_shape[1:], lambda b, t: (b, 0, 0, 0))
    else:
        assert B == 1 and tm == T and s0.shape[0] * L == T
        state_shape = s0.shape
        state_spec = pl.BlockSpec(state_shape, lambda b, t: (0, 0, 0, 0))
        in_specs.append(state_spec)
        args.append(s0)
    return pl.pallas_call(
        functools.partial(_gla_kernel, L=L, chained=chained),
        grid=(B, T // tm),
        in_specs=in_specs,
        out_specs=[pl.BlockSpec((None, tm, D), lambda b, t: (b, t, 0)), state_spec],
        out_shape=[jax.ShapeDtypeStruct((B, T, D), F32), jax.ShapeDtypeStruct(state_shape, F32)],
        compiler_params=_params(("arbitrary", "arbitrary")),
        name="gla_layer",
    )(*args)


def _ret_heads_chained(qb, kb, k, vb, s_ref, dintra_ref, dquery_ref, dkey_ref, log_decay, L):
    tm = qb.shape[0]
    o_heads = []
    for h0 in range(0, RET_HEADS, RET_LOCKSTEP):
        hs = list(range(h0, min(h0 + RET_LOCKSTEP, RET_HEADS)))
        kcs = [slice(h * RET_DK, (h + 1) * RET_DK) for h in hs]
        Ss = [s_ref[h] for h in hs]
        o_blocks = [[] for _ in hs]
        for r0 in range(0, tm, L):
            rows = slice(r0, r0 + L)
            vhs = [vb[r0:r0 + vb.shape[0] - tm + L, h * RET_DV:(h + 1) * RET_DV] for h in hs]
            atts = [(_dot_nt(qb[rows, kc], kb[rows, kc]) * dintra_ref[h]).astype(BF16) for h, kc in zip(hs, kcs)]
            os_ = [_dot(att, vh[:L]) + _dot(qb[rows, kc], S.astype(BF16))
                   * jnp.concatenate([dquery_ref[h]] * (RET_DV // LANES), axis=1)
                   for att, vh, kc, S, h in zip(atts, vhs, kcs, Ss, hs)]
            k_decs = [_pad_rows(k[rows, kc] * jnp.concatenate([dkey_ref[h]] * (RET_DK // LANES), axis=1), LANES)
                      for h, kc in zip(hs, kcs)]
            Ss = [S * math.exp(log_decay[h] * L) + _dot(k_dec.T.astype(BF16), vh)
                  for h, S, k_dec, vh in zip(hs, Ss, k_decs, vhs)]
            ocs = [o - jnp.mean(o, axis=-1, keepdims=True) for o in os_]
            for i, oc in enumerate(ocs):
                o_blocks[i].append(oc * lax.rsqrt(jnp.mean(oc * oc, axis=-1, keepdims=True) + GN_EPS))
        for i, h in enumerate(hs):
            s_ref[h] = Ss[i]
            o_heads.append(o_blocks[i][0] if len(o_blocks[i]) == 1 else jnp.concatenate(o_blocks[i], axis=0))
    return o_heads


def _ret_heads_separate(qb, kb, k, vb, s0_ref, s_ref, dintra_ref, dquery_ref, dkey_ref, log_decay, L):
    tm = qb.shape[0]
    o_heads = []
    for h0 in range(0, RET_HEADS, RET_LOCKSTEP):
        hs = list(range(h0, min(h0 + RET_LOCKSTEP, RET_HEADS)))
        kcs = [slice(h * RET_DK, (h + 1) * RET_DK) for h in hs]
        vcs = [slice(h * RET_DV, (h + 1) * RET_DV) for h in hs]
        dqs = [jnp.concatenate([dquery_ref[h]] * (RET_DV // LANES), axis=1) for h in hs]
        dks = [jnp.concatenate([dkey_ref[h]] * (RET_DK // LANES), axis=1) for h in hs]
        atts = [(_dot_nt(qb[:, kc], kb[:, kc]) * dintra_ref[h]).astype(BF16) for h, kc in zip(hs, kcs)]
        k_decs = [k[:, kc] * dk for kc, dk in zip(kcs, dks)]
        o_inters = [[] for _ in hs]
        for c in range(tm // L):
            crow = slice(c * L, (c + 1) * L)
            Ss = [s0_ref[c, h] for h in hs]
            for i in range(len(hs)):
                o_inters[i].append(_dot(qb[crow, kcs[i]], Ss[i].astype(BF16)) * dqs[i][crow])
            for h, S, k_dec, vc in zip(hs, Ss, k_decs, vcs):
                s_ref[c, h] = (S * math.exp(log_decay[h] * L)
                               + _dot(_pad_rows(k_dec[crow], LANES).T.astype(BF16), _pad_rows(vb[crow, vc], LANES)))
        os_ = [_dot(att, vb[:, vc]) + (oi[0] if len(oi) == 1 else jnp.concatenate(oi, axis=0))
               for att, vc, oi in zip(atts, vcs, o_inters)]
        ocs = [o - jnp.mean(o, axis=-1, keepdims=True) for o in os_]
        o_heads += [oc * lax.rsqrt(jnp.mean(oc * oc, axis=-1, keepdims=True) + GN_EPS) for oc in ocs]
    return o_heads


def _ret_kernel(*refs, L, chained):
    if chained:
        (x_ref, mod_ref, ng_ref, wq_ref, wk_ref, wv_ref, wg_ref, wo_ref, y_ref, s_ref,
         dintra_ref, dquery_ref, dkey_ref) = refs
    else:
        (x_ref, mod_ref, ng_ref, wq_ref, wk_ref, wv_ref, wg_ref, wo_ref, s0_ref, y_ref, s_ref,
         dintra_ref, dquery_ref, dkey_ref) = refs
    rb = dintra_ref.shape[1]
    log_decay = [math.log(1.0 - 2.0 ** (-5.0 - h)) for h in range(RET_HEADS)]

    @pl.when((pl.program_id(0) == 0) & (pl.program_id(1) == 0))
    def _():
        ri = lax.broadcasted_iota(jnp.int32, (rb, rb), 0)
        ci = lax.broadcasted_iota(jnp.int32, (rb, rb), 1)
        rel = (ri - ci).astype(F32)
        keep = (ri >= ci) & (ci >= (ri // L) * L)
        pos = (lax.broadcasted_iota(jnp.int32, (rb, LANES), 0) % L).astype(F32)
        for h in range(RET_HEADS):
            dintra_ref[h] = jnp.where(keep, jnp.exp(log_decay[h] * jnp.maximum(rel, 0.0)), 0.0)
            dquery_ref[h] = jnp.exp(log_decay[h] * (pos + 1.0))
            dkey_ref[h] = jnp.exp(log_decay[h] * (L - 1.0 - pos))

    if chained:
        @pl.when(pl.program_id(1) == 0)
        def _():
            s_ref[...] = jnp.zeros(s_ref.shape, F32)

    x = x_ref[...]
    hb = _modulated_norm(x, ng_ref[...], mod_ref[0], mod_ref[1]).astype(BF16)
    q = _dot(hb, wq_ref[...])
    k = _dot(hb, wk_ref[...]) * (RET_DK ** -0.5)
    v = _dot(hb, wv_ref[...])
    g = _dot(hb, wg_ref[...])
    qb = q.astype(BF16)
    kb = k.astype(BF16)
    if chained:
        o_heads = _ret_heads_chained(qb, kb, k, _pad_rows(v.astype(BF16), LANES), s_ref,
                                     dintra_ref, dquery_ref, dkey_ref, log_decay, L)
    else:
        o_heads = _ret_heads_separate(qb, kb, k, v.astype(BF16), s0_ref, s_ref,
                                      dintra_ref, dquery_ref, dkey_ref, log_decay, L)
    o = jnp.concatenate(o_heads, axis=1)
    o = (o * _silu(g)).astype(BF16)
    y_ref[...] = x + mod_ref[2] * _dot(o, wo_ref[...])


def _ret_layer(x, mod, ng, w, s0, tm, L):
    B, T, D = x.shape
    chained = s0 is None
    rb = L if chained else tm
    weights = [ng] + list(w)
    in_specs = [
        pl.BlockSpec((None, tm, D), lambda b, t: (b, t, 0)),
        pl.BlockSpec((None, ADA_CHUNKS, mod.shape[2], D), lambda b, t: (b, 0, 0, 0)),
    ] + [_resident(a.shape) for a in weights]
    args = [x, mod] + weights
    if chained:
        assert tm % L == 0
        state_shape = (B, RET_HEADS, RET_DK, RET_DV)
        state_spec = pl.BlockSpec((None,) + state_shape[1:], lambda b, t: (b, 0, 0, 0))
    else:
        assert tm == T and T % L == 0 and s0.shape[0] * L == B * T
        state_shape = s0.shape
        state_spec = pl.BlockSpec((T // L,) + state_shape[1:], lambda b, t: (b, 0, 0, 0))
        in_specs.append(state_spec)
        args.append(s0)
    return pl.pallas_call(
        functools.partial(_ret_kernel, L=L, chained=chained),
        grid=(B, T // tm),
        in_specs=in_specs,
        out_specs=[pl.BlockSpec((None, tm, D), lambda b, t: (b, t, 0)), state_spec],
        out_shape=[jax.ShapeDtypeStruct((B, T, D), F32), jax.ShapeDtypeStruct(state_shape, F32)],
        scratch_shapes=[pltpu.VMEM((RET_HEADS, rb, rb), F32), pltpu.VMEM((RET_HEADS, rb, LANES), F32),
                        pltpu.VMEM((RET_HEADS, rb, LANES), F32)],
        compiler_params=_params(("arbitrary", "arbitrary")),
        name="ret_layer",
    )(*args)


def _head_mean_square(a, hsum):
    w = hsum.shape[0]
    parts = [_dot((a[:, j:j + w] * a[:, j:j + w]).astype(BF16), hsum) for j in range(0, a.shape[1], w)]
    return parts[0] if len(parts) == 1 else jnp.concatenate(parts, axis=1)


def _tile_head_lanes(a, kv):
    t = a[:, (kv // 2) * LANES:(kv // 2 + 1) * LANES]
    low = lax.broadcasted_iota(jnp.int32, t.shape, 1) < SWA_HD
    keep = low if kv % 2 == 0 else jnp.logical_not(low)
    return jnp.where(keep, t, pltpu.roll(t, SWA_HD, axis=1)).astype(BF16)


def _swa_kernel(*refs, Lq, is_prompt):
    if is_prompt:
        (sink_ref, x_ref, mod_ref, ng_ref, wqkv_ref, qn_ref, kn_ref, wo_ref, y_ref, kc_ref, vc_ref,
         bias_ref) = refs
    else:
        (sink_ref, x_ref, mod_ref, ng_ref, wqkv_ref, qn_ref, kn_ref, wo_ref, ck_ref, cv_ref,
         y_ref, kc_ref, vc_ref, bias_ref) = refs
    t = pl.program_id(1)
    x = x_ref[...]
    tm = x.shape[0]
    nq = SWA_Q_HEADS * SWA_HD
    nk = SWA_KV_HEADS * SWA_HD
    gw = SWA_GROUP * SWA_HD
    nkeys = WINDOW + Lq
    nkp = nkeys + 1 + (-(nkeys + 1)) % LANES
    rows_q = SWA_GROUP * Lq

    @pl.when((pl.program_id(0) == 0) & (t == 0))
    def _():
        ri = lax.broadcasted_iota(jnp.int32, (rows_q, nkp), 0)
        ji = lax.broadcasted_iota(jnp.int32, (rows_q, nkp), 1)
        grp = ri // Lq
        dist = jnp.abs((ri - grp * Lq) - ji + WINDOW).astype(F32)
        for kv in range(SWA_KV_HEADS):
            sink = jnp.zeros((rows_q, nkp), F32)
            for gi in range(SWA_GROUP):
                sink = jnp.where(grp == gi, sink_ref[kv * SWA_GROUP + gi] * LOG2E, sink)
            bias = -LOG2E * jnp.exp2(-0.5 * (kv * SWA_GROUP + grp + 1).astype(F32)) * dist
            bias_ref[kv] = jnp.where(ji < nkeys, bias, jnp.where(ji == nkeys, sink, NEG_INF))

    hb = _modulated_norm(x, ng_ref[...], mod_ref[0], mod_ref[1]).astype(BF16)
    hi = lax.broadcasted_iota(jnp.int32, (gw, gw), 0) // SWA_HD
    hj = lax.broadcasted_iota(jnp.int32, (gw, gw), 1) // SWA_HD
    hsum = jnp.where(hi == hj, 1.0 / SWA_HD, 0.0).astype(BF16)
    kv_new = _dot(hb, wqkv_ref[:, nq:])
    k = kv_new[:, :nk]
    v_new = kv_new[:, nk:]
    k_new = k * lax.rsqrt(_head_mean_square(k, hsum) + NORM_EPS) * kn_ref[...]
    q_gain = qn_ref[...] * (SWA_HD ** -0.5 * LOG2E)
    qs_kv = []
    for kv0 in range(0, SWA_KV_HEADS, 2):
        qg = _dot(hb, wqkv_ref[:, kv0 * gw:(kv0 + 2) * gw])
        qg = (qg * lax.rsqrt(_head_mean_square(qg, hsum) + NORM_EPS) * q_gain[:, kv0 * gw:(kv0 + 2) * gw]).astype(BF16)
        qs_kv += [qg[:, :gw], qg[:, gw:]]

    nchunk = tm // Lq
    k_pad = jnp.zeros((nkp - nkeys, nk), F32)
    if is_prompt:
        @pl.when(t == 0)
        def _():
            kc_ref[...] = jnp.zeros(kc_ref.shape, F32)
            vc_ref[...] = jnp.zeros(vc_ref.shape, F32)
        k_exts = [jnp.concatenate([kc_ref[...], k_new, k_pad], axis=0)]
        v_exts = [jnp.concatenate([vc_ref[...], v_new], axis=0)]
        src = [(0, c * Lq) for c in range(nchunk)]
    else:
        k_exts = [jnp.concatenate([ck_ref[c], k_new[c * Lq:(c + 1) * Lq], k_pad], axis=0) for c in range(nchunk)]
        v_exts = [jnp.concatenate([cv_ref[c], v_new[c * Lq:(c + 1) * Lq]], axis=0) for c in range(nchunk)]
        src = [(c, 0) for c in range(nchunk)]

    low_half = lax.broadcasted_iota(jnp.int32, (Lq, LANES), 1) < SWA_HD
    ji = lax.broadcasted_iota(jnp.int32, (rows_q, nkp), 1)
    is_key = ji < nkeys
    zero = jnp.zeros((), BF16)

    o_cols = []
    for kv in range(SWA_KV_HEADS):
        kts = [_tile_head_lanes(k_ext, kv) for k_ext in k_exts]
        vts = [_tile_head_lanes(v_ext, kv) for v_ext in v_exts]
        bias = bias_ref[kv]
        o_rows = []
        for c0 in range(0, nchunk, SWA_LOCKSTEP):
            cs = list(range(c0, min(c0 + SWA_LOCKSTEP, nchunk)))
            r0s = [c * Lq for c in cs]
            kbs = [kts[src[c][0]][src[c][1]:src[c][1] + nkp] for c in cs]
            vbs = [vts[src[c][0]][src[c][1]:src[c][1] + nkeys] for c in cs]
            qss = []
            for r0 in r0s:
                qs = []
                for gi in range(SWA_GROUP):
                    qt = qs_kv[kv][r0:r0 + Lq, (gi // 2) * LANES:(gi // 2 + 1) * LANES]
                    qs.append(jnp.where(low_half if gi % 2 == 0 else jnp.logical_not(low_half), qt, zero))
                qss.append(jnp.concatenate(qs, axis=0))
            logits = [jnp.where(is_key, _dot_nt(qs, kb), 0.0) + bias for qs, kb in zip(qss, kbs)]
            if is_prompt:
                logits = [jnp.where(ji >= WINDOW - r0 - t * tm, lg, NEG_INF) if r0 < WINDOW else lg
                          for lg, r0 in zip(logits, r0s)]
            ps = [jnp.exp2(lg - jnp.max(lg, axis=-1, keepdims=True)) for lg in logits]
            dens = [jnp.sum(p, axis=-1, keepdims=True) for p in ps]
            ocs = [_dot(p[:, :nkeys].astype(BF16), vb) * (1.0 / den) for p, vb, den in zip(ps, vbs, dens)]
            for oc in ocs:
                o_rows.append(jnp.concatenate(
                    [jnp.where(low_half, oc[2 * j * Lq:(2 * j + 1) * Lq], oc[(2 * j + 1) * Lq:(2 * j + 2) * Lq])
                     for j in range(SWA_GROUP // 2)], axis=1))
        o_cols.append(o_rows[0] if len(o_rows) == 1 else jnp.concatenate(o_rows, axis=0))
    o = jnp.concatenate(o_cols, axis=1)
    y_ref[...] = x + mod_ref[2] * _dot(o.astype(BF16), wo_ref[...])
    if is_prompt:
        kc_ref[...] = k_new[tm - WINDOW:]
        vc_ref[...] = v_new[tm - WINDOW:]
    else:
        kc_ref[...] = k_new
        vc_ref[...] = v_new


def _swa_layer(x, mod, ng, w, cache, tm, Lq):
    B, T, D = x.shape
    is_prompt = cache is None
    wqkv, qn, kn, sinks, wo = w
    nk = SWA_KV_HEADS * SWA_HD
    in_specs = [
        pl.BlockSpec(memory_space=pltpu.SMEM),
        pl.BlockSpec((None, tm, D), lambda b, t: (b, t, 0)),
        pl.BlockSpec((None, ADA_CHUNKS, mod.shape[2], D), lambda b, t: (b, 0, 0, 0)),
        _resident((1, D)),
        _resident(wqkv.shape), _resident(qn.shape), _resident(kn.shape), _resident(wo.shape),
    ]
    args = [sinks, x, mod, ng, wqkv, qn, kn, wo]
    if is_prompt:
        rows_out = WINDOW
    else:
        ck, cv = cache
        assert B == 1 and tm == T and ck.shape[0] * Lq == T
        in_specs += [pl.BlockSpec(ck.shape, lambda b, t: (0, 0, 0))] * 2
        args += [ck, cv]
        rows_out = T
    cache_spec = pl.BlockSpec((None, rows_out, nk), lambda b, t: (b, 0, 0))
    nkp = WINDOW + Lq + 1 + (-(WINDOW + Lq + 1)) % LANES
    return pl.pallas_call(
        functools.partial(_swa_kernel, Lq=Lq, is_prompt=is_prompt),
        grid=(B, T // tm),
        in_specs=in_specs,
        out_specs=[pl.BlockSpec((None, tm, D), lambda b, t: (b, t, 0)), cache_spec, cache_spec],
        out_shape=[jax.ShapeDtypeStruct((B, T, D), F32),
                   jax.ShapeDtypeStruct((B, rows_out, nk), F32),
                   jax.ShapeDtypeStruct((B, rows_out, nk), F32)],
        scratch_shapes=[pltpu.VMEM((SWA_KV_HEADS, SWA_GROUP * Lq, nkp), F32)],
        compiler_params=_params(("arbitrary", "arbitrary")),
        name="swa_layer",
    )(*args)


def _tile(T, pref):
    return pref if T % pref == 0 else T


def kernel(x_prompt, x_sample, state_gla_l0, state_ret_l1, cache_swa_k_l2, cache_swa_v_l2, state_gla_l3, c_prompt, c_sample, norm_mix, norm_mlp, ada_w, ada_b, mlp_w1, mlp_w2, gla_wq, gla_wk, gla_wv, gla_wg1, gla_wg2, gla_bg, gla_wr, gla_norm, gla_wo, ret_wq, ret_wk, ret_wv, ret_wg, ret_wo, swa_wqkv, swa_qnorm, swa_knorm, swa_sinks, swa_wo):
    bp = x_prompt.shape[0]
    bs = x_sample.shape[0]
    Tp = x_prompt.shape[1]
    Ts = x_sample.shape[1]
    gla_states = [state_gla_l0, state_gla_l3]
    ret_states = [state_ret_l1]
    swa_caches = [(cache_swa_k_l2, cache_swa_v_l2)]

    mod_all = _ada_modulation(jnp.concatenate([c_prompt, c_sample], axis=0), ada_w, ada_b)
    mod_all = mod_all.reshape(DEPTH, bp + bs, ADA_CHUNKS, D_MODEL)

    w1_all = mlp_w1.astype(BF16)
    w2_all = mlp_w2.astype(BF16)
    yp, ys = x_prompt, x_sample
    new_state = []
    for layer in range(DEPTH):
        kind, j = layer % N_MIXERS, layer // N_MIXERS
        mod_p = mod_all[layer, :bp]
        mod_s = mod_all[layer, bp:]
        mod_rows = jnp.repeat(mod_s.transpose(1, 0, 2), Ts, axis=1)[None]
        ng = norm_mix[layer].reshape(1, D_MODEL)
        if kind == 0:
            wqk = jnp.concatenate([gla_wq[j].reshape(D_MODEL, GLA_HEADS, GLA_DK),
                                   gla_wk[j].reshape(D_MODEL, GLA_HEADS, GLA_DK)], axis=2).reshape(D_MODEL, -1)
            w = (wqk.astype(BF16), gla_wv[j].astype(BF16), gla_wg1[j].astype(BF16),
                 gla_wg2[j].astype(BF16), gla_bg[j].reshape(1, -1), gla_wr[j].astype(BF16),
                 gla_norm[j].reshape(1, -1), gla_wo[j].astype(BF16))
            yp, st_p = _gla_layer(yp, mod_p[:, :, None, :], ng, w, None, _tile(Tp, 1024), min(CHUNK, Tp))
            assert Ts <= CHUNK
            ys, st_s = _gla_layer(ys.reshape(1, bs * Ts, D_MODEL), mod_rows, ng, w, gla_states[j], bs * Ts, Ts)
            ys = ys.reshape(bs, Ts, D_MODEL)
            new_state += [st_p, st_s]
        elif kind == 1:
            w = (ret_wq[j].astype(BF16), ret_wk[j].astype(BF16), ret_wv[j].astype(BF16), ret_wg[j].astype(BF16),
                 ret_wo[j].astype(BF16))
            yp, st_p = _ret_layer(yp, mod_p[:, :, None, :], ng, w, None, _tile(Tp, 512), min(RET_RB, Tp))
            assert Ts <= CHUNK
            gs = RET_SAMPLE_GROUP if bs % RET_SAMPLE_GROUP == 0 else bs
            mod_g = jnp.repeat(mod_s.reshape(bs // gs, gs, ADA_CHUNKS, D_MODEL).transpose(0, 2, 1, 3), Ts, axis=2)
            ys, st_s = _ret_layer(ys.reshape(bs // gs, gs * Ts, D_MODEL), mod_g, ng, w, ret_states[j], gs * Ts, Ts)
            ys = ys.reshape(bs, Ts, D_MODEL)
            new_state += [st_p, st_s]
        else:
            w = (swa_wqkv[j].astype(BF16), jnp.tile(swa_qnorm[j], SWA_Q_HEADS).reshape(1, -1),
                 jnp.tile(swa_knorm[j], SWA_KV_HEADS).reshape(1, -1), swa_sinks[j], swa_wo[j].astype(BF16))
            yp, k_p, v_p = _swa_layer(yp, mod_p[:, :, None, :], ng, w, None, _tile(Tp, 1024), CHUNK)
            assert Ts <= CHUNK
            nk = SWA_KV_HEADS * SWA_HD
            ck, cv = (c.reshape(bs, WINDOW, nk) for c in swa_caches[j])
            ys, k_s, v_s = _swa_layer(ys.reshape(1, bs * Ts, D_MODEL), mod_rows, ng, w, (ck, cv), bs * Ts, Ts)
            ys = ys.reshape(bs, Ts, D_MODEL)
            new_state += [a.reshape(bp, WINDOW, SWA_KV_HEADS, SWA_HD) for a in (k_p, v_p)]
            new_state += [a.reshape(bs, Ts, SWA_KV_HEADS, SWA_HD) for a in (k_s, v_s)]
        gm = norm_mlp[layer].reshape(1, D_MODEL)
        yp = _mlp_layer(yp, mod_p[:, :, None, :], gm, w1_all, w2_all, layer, _tile(Tp, 1024))
        ys = _mlp_layer(ys.reshape(1, bs * Ts, D_MODEL), mod_rows, gm, w1_all, w2_all, layer,
                        bs * Ts).reshape(bs, Ts, D_MODEL)
    return (yp, ys, *new_state)
```

```python
import functools
import math

import jax
import jax.numpy as jnp
from jax import lax
from jax.experimental import pallas as pl
from jax.experimental.pallas import tpu as pltpu

F32 = jnp.float32
BF16 = jnp.bfloat16

D_MODEL = 1024
DEPTH = 4
CHUNK = 64
N_MIXERS = 3
GLA_HEADS = 4
GLA_DK = 128
GLA_DV = 256
GLA_RANK = 16
GLA_TAU = 16.0
GLA_RB = 256
GLA_LOCKSTEP = 2
RET_HEADS = 4
RET_DK = 256
RET_DV = 512
RET_RB = 256
RET_LOCKSTEP = 2
RET_SAMPLE_GROUP = 4
SWA_Q_HEADS = 16
SWA_KV_HEADS = 4
SWA_GROUP = 4
SWA_HD = 64
WINDOW = 128
SWA_LOCKSTEP = 4
MLP_HIDDEN = 4 * D_MODEL
MLP_SUB = 1024
ADA_CHUNKS = 6
NORM_EPS = 1e-6
GN_EPS = 1e-5
NEG_INF = -1e30
LOG2E = 1.4426950408889634

VMEM_LIMIT_BYTES = 56 * 1024 * 1024
LANES = 128


def _dot(a, b):
    return jnp.dot(a, b, preferred_element_type=F32)


def _dot_nt(a, b):
    return lax.dot_general(a, b, (((1,), (1,)), ((), ())), preferred_element_type=F32)


def _pad_rows(a, mult):
    r = (-a.shape[0]) % mult
    return a if r == 0 else jnp.concatenate([a, jnp.zeros((r, a.shape[1]), a.dtype)], axis=0)


def _modulated_norm(x, g, shift, scale):
    ms = jnp.mean(x * x, axis=-1, keepdims=True)
    return x * lax.rsqrt(ms + NORM_EPS) * (g * (1.0 + scale)) + shift


def _silu(x):
    return x * (1.0 / (1.0 + jnp.exp2(x * -LOG2E)))


def _resident(shape):
    nd = len(shape)
    return pl.BlockSpec(shape, lambda *_: (0,) * nd, pipeline_mode=pl.Buffered(1))


def _params(sem):
    return pltpu.CompilerParams(dimension_semantics=sem, vmem_limit_bytes=VMEM_LIMIT_BYTES)


def _ada_kernel(c_ref, w_ref, b_ref, o_ref):
    c = c_ref[...]
    a = _silu(c).astype(BF16)
    o_ref[...] = _dot(a, w_ref[...].astype(BF16)) + b_ref[...]


def _ada_modulation(c_all, ada_w, ada_b):
    nb = c_all.shape[0]
    width = ADA_CHUNKS * D_MODEL
    tn = 1536
    return pl.pallas_call(
        _ada_kernel,
        grid=(DEPTH, width // tn),
        in_specs=[
            pl.BlockSpec((nb, D_MODEL), lambda l, j: (0, 0)),
            pl.BlockSpec((None, D_MODEL, tn), lambda l, j: (l, 0, j)),
            pl.BlockSpec((None, 1, tn), lambda l, j: (l, 0, j)),
        ],
        out_specs=pl.BlockSpec((None, nb, tn), lambda l, j: (l, 0, j)),
        out_shape=jax.ShapeDtypeStruct((DEPTH, nb, width), F32),
        compiler_params=_params(("arbitrary", "arbitrary")),
        name="ada_modulation",
    )(c_all, ada_w, ada_b.reshape(DEPTH, 1, width))


def _mlp_kernel(x_ref, mod_ref, g_ref, w1_ref, w2_ref, o_ref, *, th):
    tm = x_ref.shape[0]
    sub = min(tm, MLP_SUB)
    for r0 in range(0, tm, sub):
        rows = slice(r0, r0 + sub)
        mod = [mod_ref[i] if mod_ref.shape[1] == 1 else mod_ref[i, rows, :] for i in range(ADA_CHUNKS)]
        x = x_ref[rows, :]
        h = _modulated_norm(x, g_ref[...], mod[3], mod[4]).astype(BF16)
        acc = jnp.zeros(x.shape, F32)
        for c in range(MLP_HIDDEN // th):
            a = _dot(h, w1_ref[:, c * th:(c + 1) * th])
            a = jnp.maximum(a, 0.0)
            a = a * a
            acc = acc + _dot(a.astype(BF16), w2_ref[c * th:(c + 1) * th, :])
        o_ref[rows, :] = x + mod[5] * acc


def _mlp_layer(x, mod, g, w1_all, w2_all, layer, tm):
    B, T, D = x.shape
    return pl.pallas_call(
        functools.partial(_mlp_kernel, th=512),
        grid=(B, T // tm),
        in_specs=[
            pl.BlockSpec((None, tm, D), lambda b, t: (b, t, 0)),
            pl.BlockSpec((None, ADA_CHUNKS, mod.shape[2], D), lambda b, t: (b, 0, 0, 0)),
            _resident((1, D)),
            pl.BlockSpec((None, D, MLP_HIDDEN), lambda b, t: (layer, 0, 0), pipeline_mode=pl.Buffered(1)),
            pl.BlockSpec((None, MLP_HIDDEN, D), lambda b, t: (layer, 0, 0), pipeline_mode=pl.Buffered(1)),
        ],
        out_specs=pl.BlockSpec((None, tm, D), lambda b, t: (b, t, 0)),
        out_shape=jax.ShapeDtypeStruct((B, T, D), F32),
        compiler_params=_params(("arbitrary", "arbitrary")),
        name="mlp_layer",
    )(x, mod, g, w1_all, w2_all)


def _chunk_cumsum(a, L):
    row = lax.broadcasted_iota(jnp.int32, a.shape, 0) % L
    s = 1
    while s < L:
        a = a + jnp.where(row >= s, pltpu.roll(a, s, axis=0), 0.0)
        s *= 2
    return a


def _gla_kernel(*refs, L, chained):
    if chained:
        (x_ref, mod_ref, ng_ref, wqk_ref, wv_ref, wg1_ref, wg2_ref, bg_ref, wr_ref, gn_ref, wo_ref,
         y_ref, s_ref) = refs

        @pl.when(pl.program_id(1) == 0)
        def _():
            s_ref[...] = jnp.zeros(s_ref.shape, F32)
    else:
        (x_ref, mod_ref, ng_ref, wqk_ref, wv_ref, wg1_ref, wg2_ref, bg_ref, wr_ref, gn_ref, wo_ref,
         s0_ref, y_ref, s_ref) = refs

    x = x_ref[...]
    tm = x.shape[0]
    nc = tm // L
    hb = _modulated_norm(x, ng_ref[...], mod_ref[0], mod_ref[1]).astype(BF16)
    g1 = _dot(hb, wg1_ref[...]).astype(BF16)
    rb = min(tm, GLA_RB)
    ncb = rb // L
    ri = lax.broadcasted_iota(jnp.int32, (rb, rb), 0)
    ci = lax.broadcasted_iota(jnp.int32, (rb, rb), 1)
    mask = ri >= ci
    if ncb > 1:
        mask = mask & (ci >= (ri // L) * L)
    rbp = rb + (-rb) % LANES
    lane_chunk = lax.broadcasted_iota(jnp.int32, (GLA_DK, rbp), 1) // L
    gn = gn_ref[...]

    o_heads = []
    for h0 in range(0, GLA_HEADS, GLA_LOCKSTEP):
        hs = list(range(h0, min(h0 + GLA_LOCKSTEP, GLA_HEADS)))
        kcs = [slice(h * GLA_DK, (h + 1) * GLA_DK) for h in hs]
        vcs = [slice(h * GLA_DV, (h + 1) * GLA_DV) for h in hs]
        zs = [_dot(g1, wg2_ref[:, kc]) + bg_ref[:, kc] for kc in kcs]
        loggs = [(jnp.minimum(z, 0.0) - jnp.log(1.0 + jnp.exp(-jnp.abs(z)))) * (1.0 / GLA_TAU) for z in zs]
        bs = [_chunk_cumsum(lg, L) for lg in loggs]
        ebs = [jnp.exp(b) for b in bs]
        enbs = [jnp.exp(-b) for b in bs]
        ebls = [[jnp.exp(b[c * L + L - 1:c * L + L, :]) for c in range(nc)] for b in bs]
        ebl_rows = [[jnp.broadcast_to(e, (L, GLA_DK)) for e in ebl] for ebl in ebls]
        ebl_rows = [r[0] if nc == 1 else jnp.concatenate(r, axis=0) for r in ebl_rows]
        qks = [_dot(hb, wqk_ref[:, 2 * h * GLA_DK:2 * (h + 1) * GLA_DK]) for h in hs]
        qds = [(qk[:, :GLA_DK] * (eb * GLA_DK ** -0.5)).astype(BF16) for qk, eb in zip(qks, ebs)]
        kdfs = [qk[:, GLA_DK:] * enb for qk, enb in zip(qks, enbs)]
        kds = [kdf.astype(BF16) for kdf in kdfs]
        kdecs = [kdf * r for kdf, r in zip(kdfs, ebl_rows)]
        vs = [_dot(hb, wv_ref[:, vc]).astype(BF16) for vc in vcs]
        gates = [_silu(_dot(hb, wr_ref[:, vc])) for vc in vcs]
        Ss = [s_ref[h] for h in hs] if chained else None
        o_blocks = [[] for _ in hs]
        for r0 in range(0, tm, rb):
            rows = slice(r0, r0 + rb)
            vhs = [_pad_rows(v[rows], LANES) for v in vs]
            atts = [jnp.where(mask, _dot_nt(qd[rows], kd[rows]), 0.0).astype(BF16) for qd, kd in zip(qds, kds)]
            os_ = [_dot(att, vh[:rb]) for att, vh in zip(atts, vhs)]
            kdec_ts = [_pad_rows(kdec[rows], LANES).T for kdec in kdecs]
            lhss = [[jnp.where(lane_chunk == c, kdec_t, 0.0) for c in range(ncb)] for kdec_t in kdec_ts]
            lhss = [(l[0] if ncb == 1 else jnp.concatenate(l, axis=0)).astype(BF16) for l in lhss]
            us = [_dot(lhs, vh) for lhs, vh in zip(lhss, vhs)]
            o_inters = [[] for _ in hs]
            for c in range(ncb):
                if not chained:
                    Ss = [s0_ref[r0 // L + c, h] for h in hs]
                for i in range(len(hs)):
                    o_inters[i].append(_dot(qds[i][r0 + c * L:r0 + (c + 1) * L], Ss[i].astype(BF16)))
                dcols = [jnp.broadcast_to(ebl[r0 // L + c], (GLA_DK, GLA_DK)).T for ebl in ebls]
                Ss = [S * jnp.concatenate([dcol, dcol], axis=1) + u[c * GLA_DK:(c + 1) * GLA_DK]
                      for S, dcol, u in zip(Ss, dcols, us)]
                if not chained:
                    for i, h in enumerate(hs):
                        s_ref[r0 // L + c, h] = Ss[i]
            for i in range(len(hs)):
                oi = o_inters[i]
                o_blocks[i].append(os_[i] + (oi[0] if ncb == 1 else jnp.concatenate(oi, axis=0)))
        for i, h in enumerate(hs):
            if chained:
                s_ref[h] = Ss[i]
            o = o_blocks[i][0] if len(o_blocks[i]) == 1 else jnp.concatenate(o_blocks[i], axis=0)
            o = o * lax.rsqrt(jnp.mean(o * o, axis=-1, keepdims=True) + NORM_EPS) * gn
            o_heads.append((o * gates[i]).astype(BF16))
    y_ref[...] = x + mod_ref[2] * _dot(jnp.concatenate(o_heads, axis=1), wo_ref[...])


def _gla_layer(x, mod, ng, w, s0, tm, L):
    B, T, D = x.shape
    chained = s0 is None
    weights = [ng] + list(w)
    in_specs = [
        pl.BlockSpec((None, tm, D), lambda b, t: (b, t, 0)),
        pl.BlockSpec((None, ADA_CHUNKS, mod.shape[2], D), lambda b, t: (b, 0, 0, 0)),
    ] + [_resident(a.shape) for a in weights]
    args = [x, mod] + weights
    if chained:
        state_shape = (B, GLA_HEADS, GLA_DK, GLA_DV)
        state_spec = pl.BlockSpec((None,) + state_shape[1:], lambda b, t: (b, 0, 0, 0))
    else:
        assert B == 1 and tm == T and s0.shape[0] * L == T
        state_shape = s0.shape
        state_spec = pl.BlockSpec(state_shape, lambda b, t: (0, 0, 0, 0))
        in_specs.append(state_spec)
        args.append(s0)
    return pl.pallas_call(
        functools.partial(_gla_kernel, L=L, chained=chained),
        grid=(B, T // tm),
        in_specs=in_specs,
        out_specs=[pl.BlockSpec((None, tm, D), lambda b, t: (b, t, 0)), state_spec],
        out_shape=[jax.ShapeDtypeStruct((B, T, D), F32), jax.ShapeDtypeStruct(state_shape, F32)],
        compiler_params=_params(("arbitrary", "arbitrary")),
        name="gla_layer",
    )(*args)


def _ret_heads_chained(qb, kb, k, vb, s_ref, dintra_ref, dquery_ref, dkey_ref, log_decay, L):
    tm = qb.shape[0]
    o_heads = []
    for h0 in range(0, RET_HEADS, RET_LOCKSTEP):
        hs = list(range(h0, min(h0 + RET_LOCKSTEP, RET_HEADS)))
        kcs = [slice(h * RET_DK, (h + 1) * RET_DK) for h in hs]
        Ss = [s_ref[h] for h in hs]
        o_blocks = [[] for _ in hs]
        for r0 in range(0, tm, L):
            rows = slice(r0, r0 + L)
            vhs = [vb[r0:r0 + vb.shape[0] - tm + L, h * RET_DV:(h + 1) * RET_DV] for h in hs]
            atts = [(_dot_nt(qb[rows, kc], kb[rows, kc]) * dintra_ref[h]).astype(BF16) for h, kc in zip(hs, kcs)]
            os_ = [_dot(att, vh[:L]) + _dot(qb[rows, kc], S.astype(BF16))
                   * jnp.concatenate([dquery_ref[h]] * (RET_DV // LANES), axis=1)
                   for att, vh, kc, S, h in zip(atts, vhs, kcs, Ss, hs)]
            k_decs = [_pad_rows(k[rows, kc] * jnp.concatenate([dkey_ref[h]] * (RET_DK // LANES), axis=1), LANES)
                      for h, kc in zip(hs, kcs)]
            Ss = [S * math.exp(log_decay[h] * L) + _dot(k_dec.T.astype(BF16), vh)
                  for h, S, k_dec, vh in zip(hs, Ss, k_decs, vhs)]
            ocs = [o - jnp.mean(o, axis=-1, keepdims=True) for o in os_]
            for i, oc in enumerate(ocs):
                o_blocks[i].append(oc * lax.rsqrt(jnp.mean(oc * oc, axis=-1, keepdims=True) + GN_EPS))
        for i, h in enumerate(hs):
            s_ref[h] = Ss[i]
            o_heads.append(o_blocks[i][0] if len(o_blocks[i]) == 1 else jnp.concatenate(o_blocks[i], axis=0))
    return o_heads


def _ret_heads_separate(qb, kb, k, vb, s0_ref, s_ref, dintra_ref, dquery_ref, dkey_ref, log_decay, L):
    tm = qb.shape[0]
    o_heads = []
    for h0 in range(0, RET_HEADS, RET_LOCKSTEP):
        hs = list(range(h0, min(h0 + RET_LOCKSTEP, RET_HEADS)))
        kcs = [slice(h * RET_DK, (h + 1) * RET_DK) for h in hs]
        vcs = [slice(h * RET_DV, (h + 1) * RET_DV) for h in hs]
        dqs = [jnp.concatenate([dquery_ref[h]] * (RET_DV // LANES), axis=1) for h in hs]
        dks = [jnp.concatenate([dkey_ref[h]] * (RET_DK // LANES), axis=1) for h in hs]
        atts = [(_dot_nt(qb[:, kc], kb[:, kc]) * dintra_ref[h]).astype(BF16) for h, kc in zip(hs, kcs)]
        k_decs = [k[:, kc] * dk for kc, dk in zip(kcs, dks)]
        o_inters = [[] for _ in hs]
        for c in range(tm // L):
            crow = slice(c * L, (c + 1) * L)
            Ss = [s0_ref[c, h] for h in hs]
            for i in range(len(hs)):
                o_inters[i].append(_dot(qb[crow, kcs[i]], Ss[i].astype(BF16)) * dqs[i][crow])
            for h, S, k_dec, vc in zip(hs, Ss, k_decs, vcs):
                s_ref[c, h] = (S * math.exp(log_decay[h] * L)
                               + _dot(_pad_rows(k_dec[crow], LANES).T.astype(BF16), _pad_rows(vb[crow, vc], LANES)))
        os_ = [_dot(att, vb[:, vc]) + (oi[0] if len(oi) == 1 else jnp.concatenate(oi, axis=0))
               for att, vc, oi in zip(atts, vcs, o_inters)]
        ocs = [o - jnp.mean(o, axis=-1, keepdims=True) for o in os_]
        o_heads += [oc * lax.rsqrt(jnp.mean(oc * oc, axis=-1, keepdims=True) + GN_EPS) for oc in ocs]
    return o_heads


def _ret_kernel(*refs, L, chained):
    if chained:
        (x_ref, mod_ref, ng_ref, wq_ref, wk_ref, wv_ref, wg_ref, wo_ref, y_ref, s_ref,
         dintra_ref, dquery_ref, dkey_ref) = refs
    else:
        (x_ref, mod_ref, ng_ref, wq_ref, wk_ref, wv_ref, wg_ref, wo_ref, s0_ref, y_ref, s_ref,
         dintra_ref, dquery_ref, dkey_ref) = refs
    rb = dintra_ref.shape[1]
    log_decay = [math.log(1.0 - 2.0 ** (-5.0 - h)) for h in range(RET_HEADS)]

    @pl.when((pl.program_id(0) == 0) & (pl.program_id(1) == 0))
    def _():
        ri = lax.broadcasted_iota(jnp.int32, (rb, rb), 0)
        ci = lax.broadcasted_iota(jnp.int32, (rb, rb), 1)
        rel = (ri - ci).astype(F32)
        keep = (ri >= ci) & (ci >= (ri // L) * L)
        pos = (lax.broadcasted_iota(jnp.int32, (rb, LANES), 0) % L).astype(F32)
        for h in range(RET_HEADS):
            dintra_ref[h] = jnp.where(keep, jnp.exp(log_decay[h] * jnp.maximum(rel, 0.0)), 0.0)
            dquery_ref[h] = jnp.exp(log_decay[h] * (pos + 1.0))
            dkey_ref[h] = jnp.exp(log_decay[h] * (L - 1.0 - pos))

    if chained:
        @pl.when(pl.program_id(1) == 0)
        def _():
            s_ref[...] = jnp.zeros(s_ref.shape, F32)

    x = x_ref[...]
    hb = _modulated_norm(x, ng_ref[...], mod_ref[0], mod_ref[1]).astype(BF16)
    q = _dot(hb, wq_ref[...])
    k = _dot(hb, wk_ref[...]) * (RET_DK ** -0.5)
    v = _dot(hb, wv_ref[...])
    g = _dot(hb, wg_ref[...])
    qb = q.astype(BF16)
    kb = k.astype(BF16)
    if chained:
        o_heads = _ret_heads_chained(qb, kb, k, _pad_rows(v.astype(BF16), LANES), s_ref,
                                     dintra_ref, dquery_ref, dkey_ref, log_decay, L)
    else:
        o_heads = _ret_heads_separate(qb, kb, k, v.astype(BF16), s0_ref, s_ref,
                                      dintra_ref, dquery_ref, dkey_ref, log_decay, L)
    o = jnp.concatenate(o_heads, axis=1)
    o = (o * _silu(g)).astype(BF16)
    y_ref[...] = x + mod_ref[2] * _dot(o, wo_ref[...])


def _ret_layer(x, mod, ng, w, s0, tm, L):
    B, T, D = x.shape
    chained = s0 is None
    rb = L if chained else tm
    weights = [ng] + list(w)
    in_specs = [
        pl.BlockSpec((None, tm, D), lambda b, t: (b, t, 0)),
        pl.BlockSpec((None, ADA_CHUNKS, mod.shape[2], D), lambda b, t: (b, 0, 0, 0)),
    ] + [_resident(a.shape) for a in weights]
    args = [x, mod] + weights
    if chained:
        assert tm % L == 0
        state_shape = (B, RET_HEADS, RET_DK, RET_DV)
        state_spec = pl.BlockSpec((None,) + state_shape[1:], lambda b, t: (b, 0, 0, 0))
    else:
        assert tm == T and T % L == 0 and s0.shape[0] * L == B * T
        state_shape = s0.shape
        state_spec = pl.BlockSpec((T // L,) + state_shape[1:], lambda b, t: (b, 0, 0, 0))
        in_specs.append(state_spec)
        args.append(s0)
    return pl.pallas_call(
        functools.partial(_ret_kernel, L=L, chained=chained),
        grid=(B, T // tm),
        in_specs=in_specs,
        out_specs=[pl.BlockSpec((None, tm, D), lambda b, t: (b, t, 0)), state_spec],
        out_shape=[jax.ShapeDtypeStruct((B, T, D), F32), jax.ShapeDtypeStruct(state_shape, F32)],
        scratch_shapes=[pltpu.VMEM((RET_HEADS, rb, rb), F32), pltpu.VMEM((RET_HEADS, rb, LANES), F32),
                        pltpu.VMEM((RET_HEADS, rb, LANES), F32)],
        compiler_params=_params(("arbitrary", "arbitrary")),
        name="ret_layer",
    )(*args)


def _head_mean_square(a, hsum):
    w = hsum.shape[0]
    parts = [_dot((a[:, j:j + w] * a[:, j:j + w]).astype(BF16), hsum) for j in range(0, a.shape[1], w)]
    return parts[0] if len(parts) == 1 else jnp.concatenate(parts, axis=1)


def _tile_head_lanes(a, kv):
    t = a[:, (kv // 2) * LANES:(kv // 2 + 1) * LANES]
    low = lax.broadcasted_iota(jnp.int32, t.shape, 1) < SWA_HD
    keep = low if kv % 2 == 0 else jnp.logical_not(low)
    return jnp.where(keep, t, pltpu.roll(t, SWA_HD, axis=1)).astype(BF16)


def _swa_kernel(*refs, Lq, is_prompt):
    if is_prompt:
        (sink_ref, x_ref, mod_ref, ng_ref, wqkv_ref, qn_ref, kn_ref, wo_ref, y_ref, kc_ref, vc_ref,
         bias_ref) = refs
    else:
        (sink_ref, x_ref, mod_ref, ng_ref, wqkv_ref, qn_ref, kn_ref, wo_ref, ck_ref, cv_ref,
         y_ref, kc_ref, vc_ref, bias_ref) = refs
    t = pl.program_id(1)
    x = x_ref[...]
    tm = x.shape[0]
    nq = SWA_Q_HEADS * SWA_HD
    nk = SWA_KV_HEADS * SWA_HD
    gw = SWA_GROUP * SWA_HD
    nkeys = WINDOW + Lq
    nkp = nkeys + 1 + (-(nkeys + 1)) % LANES
    rows_q = SWA_GROUP * Lq

    @pl.when((pl.program_id(0) == 0) & (t == 0))
    def _():
        ri = lax.broadcasted_iota(jnp.int32, (rows_q, nkp), 0)
        ji = lax.broadcasted_iota(jnp.int32, (rows_q, nkp), 1)
        grp = ri // Lq
        dist = jnp.abs((ri - grp * Lq) - ji + WINDOW).astype(F32)
        for kv in range(SWA_KV_HEADS):
            sink = jnp.zeros((rows_q, nkp), F32)
            for gi in range(SWA_GROUP):
                sink = jnp.where(grp == gi, sink_ref[kv * SWA_GROUP + gi] * LOG2E, sink)
            bias = -LOG2E * jnp.exp2(-0.5 * (kv * SWA_GROUP + grp + 1).astype(F32)) * dist
            bias_ref[kv] = jnp.where(ji < nkeys, bias, jnp.where(ji == nkeys, sink, NEG_INF))

    hb = _modulated_norm(x, ng_ref[...], mod_ref[0], mod_ref[1]).astype(BF16)
    hi = lax.broadcasted_iota(jnp.int32, (gw, gw), 0) // SWA_HD
    hj = lax.broadcasted_iota(jnp.int32, (gw, gw), 1) // SWA_HD
    hsum = jnp.where(hi == hj, 1.0 / SWA_HD, 0.0).astype(BF16)
    kv_new = _dot(hb, wqkv_ref[:, nq:])
    k = kv_new[:, :nk]
    v_new = kv_new[:, nk:]
    k_new = k * lax.rsqrt(_head_mean_square(k, hsum) + NORM_EPS) * kn_ref[...]
    q_gain = qn_ref[...] * (SWA_HD ** -0.5 * LOG2E)
    qs_kv = []
    for kv0 in range(0, SWA_KV_HEADS, 2):
        qg = _dot(hb, wqkv_ref[:, kv0 * gw:(kv0 + 2) * gw])
        qg = (qg * lax.rsqrt(_head_mean_square(qg, hsum) + NORM_EPS) * q_gain[:, kv0 * gw:(kv0 + 2) * gw]).astype(BF16)
        qs_kv += [qg[:, :gw], qg[:, gw:]]

    nchunk = tm // Lq
    k_pad = jnp.zeros((nkp - nkeys, nk), F32)
    if is_prompt:
        @pl.when(t == 0)
        def _():
            kc_ref[...] = jnp.zeros(kc_ref.shape, F32)
            vc_ref[...] = jnp.zeros(vc_ref.shape, F32)
        k_exts = [jnp.concatenate([kc_ref[...], k_new, k_pad], axis=0)]
        v_exts = [jnp.concatenate([vc_ref[...], v_new], axis=0)]
        src = [(0, c * Lq) for c in range(nchunk)]
    else:
        k_exts = [jnp.concatenate([ck_ref[c], k_new[c * Lq:(c + 1) * Lq], k_pad], axis=0) for c in range(nchunk)]
        v_exts = [jnp.concatenate([cv_ref[c], v_new[c * Lq:(c + 1) * Lq]], axis=0) for c in range(nchunk)]
        src = [(c, 0) for c in range(nchunk)]

    low_half = lax.broadcasted_iota(jnp.int32, (Lq, LANES), 1) < SWA_HD
    ji = lax.broadcasted_iota(jnp.int32, (rows_q, nkp), 1)
    is_key = ji < nkeys
    zero = jnp.zeros((), BF16)

    o_cols = []
    for kv in range(SWA_KV_HEADS):
        kts = [_tile_head_lanes(k_ext, kv) for k_ext in k_exts]
        vts = [_tile_head_lanes(v_ext, kv) for v_ext in v_exts]
        bias = bias_ref[kv]
        o_rows = []
        for c0 in range(0, nchunk, SWA_LOCKSTEP):
            cs = list(range(c0, min(c0 + SWA_LOCKSTEP, nchunk)))
            r0s = [c * Lq for c in cs]
            kbs = [kts[src[c][0]][src[c][1]:src[c][1] + nkp] for c in cs]
            vbs = [vts[src[c][0]][src[c][1]:src[c][1] + nkeys] for c in cs]
            qss = []
            for r0 in r0s:
                qs = []
                for gi in range(SWA_GROUP):
                    qt = qs_kv[kv][r0:r0 + Lq, (gi // 2) * LANES:(gi // 2 + 1) * LANES]
                    qs.append(jnp.where(low_half if gi % 2 == 0 else jnp.logical_not(low_half), qt, zero))
                qss.append(jnp.concatenate(qs, axis=0))
            logits = [jnp.where(is_key, _dot_nt(qs, kb), 0.0) + bias for qs, kb in zip(qss, kbs)]
            if is_prompt:
                logits = [jnp.where(ji >= WINDOW - r0 - t * tm, lg, NEG_INF) if r0 < WINDOW else lg
                          for lg, r0 in zip(logits, r0s)]
            ps = [jnp.exp2(lg - jnp.max(lg, axis=-1, keepdims=True)) for lg in logits]
            dens = [jnp.sum(p, axis=-1, keepdims=True) for p in ps]
            ocs = [_dot(p[:, :nkeys].astype(BF16), vb) * (1.0 / den) for p, vb, den in zip(ps, vbs, dens)]
            for oc in ocs:
                o_rows.append(jnp.concatenate(
                    [jnp.where(low_half, oc[2 * j * Lq:(2 * j + 1) * Lq], oc[(2 * j + 1) * Lq:(2 * j + 2) * Lq])
                     for j in range(SWA_GROUP // 2)], axis=1))
        o_cols.append(o_rows[0] if len(o_rows) == 1 else jnp.concatenate(o_rows, axis=0))
    o = jnp.concatenate(o_cols, axis=1)
    y_ref[...] = x + mod_ref[2] * _dot(o.astype(BF16), wo_ref[...])
    if is_prompt:
        kc_ref[...] = k_new[tm - WINDOW:]
        vc_ref[...] = v_new[tm - WINDOW:]
    else:
        kc_ref[...] = k_new
        vc_ref[...] = v_new


def _swa_layer(x, mod, ng, w, cache, tm, Lq):
    B, T, D = x.shape
    is_prompt = cache is None
    wqkv, qn, kn, sinks, wo = w
    nk = SWA_KV_HEADS * SWA_HD
    in_specs = [
        pl.BlockSpec(memory_space=pltpu.SMEM),
        pl.BlockSpec((None, tm, D), lambda b, t: (b, t, 0)),
        pl.BlockSpec((None, ADA_CHUNKS, mod.shape[2], D), lambda b, t: (b, 0, 0, 0)),
        _resident((1, D)),
        _resident(wqkv.shape), _resident(qn.shape), _resident(kn.shape), _resident(wo.shape),
    ]
    args = [sinks, x, mod, ng, wqkv, qn, kn, wo]
    if is_prompt:
        rows_out = WINDOW
    else:
        ck, cv = cache
        assert B == 1 and tm == T and ck.shape[0] * Lq == T
        in_specs += [pl.BlockSpec(ck.shape, lambda b, t: (0, 0, 0))] * 2
        args += [ck, cv]
        rows_out = T
    cache_spec = pl.BlockSpec((None, rows_out, nk), lambda b, t: (b, 0, 0))
    nkp = WINDOW + Lq + 1 + (-(WINDOW + Lq + 1)) % LANES
    return pl.pallas_call(
        functools.partial(_swa_kernel, Lq=Lq, is_prompt=is_prompt),
        grid=(B, T // tm),
        in_specs=in_specs,
        out_specs=[pl.BlockSpec((None, tm, D), lambda b, t: (b, t, 0)), cache_spec, cache_spec],
        out_shape=[jax.ShapeDtypeStruct((B, T, D), F32),
                   jax.ShapeDtypeStruct((B, rows_out, nk), F32),
                   jax.ShapeDtypeStruct((B, rows_out, nk), F32)],
        scratch_shapes=[pltpu.VMEM((SWA_KV_HEADS, SWA_GROUP * Lq, nkp), F32)],
        compiler_params=_params(("arbitrary", "arbitrary")),
        name="swa_layer",
    )(*args)


def _tile(T, pref):
    return pref if T % pref == 0 else T


def kernel(x_prompt, x_sample, state_gla_l0, state_ret_l1, cache_swa_k_l2, cache_swa_v_l2, state_gla_l3, c_prompt, c_sample, norm_mix, norm_mlp, ada_w, ada_b, mlp_w1, mlp_w2, gla_wq, gla_wk, gla_wv, gla_wg1, gla_wg2, gla_bg, gla_wr, gla_norm, gla_wo, ret_wq, ret_wk, ret_wv, ret_wg, ret_wo, swa_wqkv, swa_qnorm, swa_knorm, swa_sinks, swa_wo):
    bp = x_prompt.shape[0]
    bs = x_sample.shape[0]
    Tp = x_prompt.shape[1]
    Ts = x_sample.shape[1]
    gla_states = [state_gla_l0, state_gla_l3]
    ret_states = [state_ret_l1]
    swa_caches = [(cache_swa_k_l2, cache_swa_v_l2)]

    mod_all = _ada_modulation(jnp.concatenate([c_prompt, c_sample], axis=0), ada_w, ada_b)
    mod_all = mod_all.reshape(DEPTH, bp + bs, ADA_CHUNKS, D_MODEL)

    w1_all = mlp_w1.astype(BF16)
    w2_all = mlp_w2.astype(BF16)
    yp, ys = x_prompt, x_sample
    new_state = []
    for layer in range(DEPTH):
        kind, j = layer % N_MIXERS, layer // N_MIXERS
        mod_p = mod_all[layer, :bp]
        mod_s = mod_all[layer, bp:]
        mod_rows = jnp.repeat(mod_s.transpose(1, 0, 2), Ts, axis=1)[None]
        ng = norm_mix[layer].reshape(1, D_MODEL)
        if kind == 0:
            wqk = jnp.concatenate([gla_wq[j].reshape(D_MODEL, GLA_HEADS, GLA_DK),
                                   gla_wk[j].reshape(D_MODEL, GLA_HEADS, GLA_DK)], axis=2).reshape(D_MODEL, -1)
            w = (wqk.astype(BF16), gla_wv[j].astype(BF16), gla_wg1[j].astype(BF16),
                 gla_wg2[j].astype(BF16), gla_bg[j].reshape(1, -1), gla_wr[j].astype(BF16),
                 gla_norm[j].reshape(1, -1), gla_wo[j].astype(BF16))
            yp, st_p = _gla_layer(yp, mod_p[:, :, None, :], ng, w, None, _tile(Tp, 1024), min(CHUNK, Tp))
            assert Ts <= CHUNK
            ys, st_s = _gla_layer(ys.reshape(1, bs * Ts, D_MODEL), mod_rows, ng, w, gla_states[j], bs * Ts, Ts)
            ys = ys.reshape(bs, Ts, D_MODEL)
            new_state += [st_p, st_s]
        elif kind == 1:
            w = (ret_wq[j].astype(BF16), ret_wk[j].astype(BF16), ret_wv[j].astype(BF16), ret_wg[j].astype(BF16),
                 ret_wo[j].astype(BF16))
            yp, st_p = _ret_layer(yp, mod_p[:, :, None, :], ng, w, None, _tile(Tp, 512), min(RET_RB, Tp))
            assert Ts <= CHUNK
            gs = RET_SAMPLE_GROUP if bs % RET_SAMPLE_GROUP == 0 else bs
            mod_g = jnp.repeat(mod_s.reshape(bs // gs, gs, ADA_CHUNKS, D_MODEL).transpose(0, 2, 1, 3), Ts, axis=2)
            ys, st_s = _ret_layer(ys.reshape(bs // gs, gs * Ts, D_MODEL), mod_g, ng, w, ret_states[j], gs * Ts, Ts)
            ys = ys.reshape(bs, Ts, D_MODEL)
            new_state += [st_p, st_s]
        else:
            w = (swa_wqkv[j].astype(BF16), jnp.tile(swa_qnorm[j], SWA_Q_HEADS).reshape(1, -1),
                 jnp.tile(swa_knorm[j], SWA_KV_HEADS).reshape(1, -1), swa_sinks[j], swa_wo[j].astype(BF16))
            yp, k_p, v_p = _swa_layer(yp, mod_p[:, :, None, :], ng, w, None, _tile(Tp, 1024), CHUNK)
            assert Ts <= CHUNK
            nk = SWA_KV_HEADS * SWA_HD
            ck, cv = (c.reshape(bs, WINDOW, nk) for c in swa_caches[j])
            ys, k_s, v_s = _swa_layer(ys.reshape(1, bs * Ts, D_MODEL), mod_rows, ng, w, (ck, cv), bs * Ts, Ts)
            ys = ys.reshape(bs, Ts, D_MODEL)
            new_state += [a.reshape(bp, WINDOW, SWA_KV_HEADS, SWA_HD) for a in (k_p, v_p)]
            new_state += [a.reshape(bs, Ts, SWA_KV_HEADS, SWA_HD) for a in (k_s, v_s)]
        gm = norm_mlp[layer].reshape(1, D_MODEL)
        yp = _mlp_layer(yp, mod_p[:, :, None, :], gm, w1_all, w2_all, layer, _tile(Tp, 2048))
        ys = _mlp_layer(ys.reshape(1, bs * Ts, D_MODEL), mod_rows, gm, w1_all, w2_all, layer,
                        bs * Ts).reshape(bs, Ts, D_MODEL)
    return (yp, ys, *new_state)
```

```python
import functools
import math

import jax
import jax.numpy as jnp
from jax import lax
from jax.experimental import pallas as pl
from jax.experimental.pallas import tpu as pltpu

F32 = jnp.float32
BF16 = jnp.bfloat16

D_MODEL = 1024
DEPTH = 4
CHUNK = 64
N_MIXERS = 3
GLA_HEADS = 4
GLA_DK = 128
GLA_DV = 256
GLA_RANK = 16
GLA_TAU = 16.0
GLA_RB = 256
GLA_LOCKSTEP = 2
RET_HEADS = 4
RET_DK = 256
RET_DV = 512
RET_RB = 256
RET_LOCKSTEP = 2
RET_SAMPLE_GROUP = 4
SWA_Q_HEADS = 16
SWA_KV_HEADS = 4
SWA_GROUP = 4
SWA_HD = 64
WINDOW = 128
SWA_LOCKSTEP = 4
MLP_HIDDEN = 4 * D_MODEL
ADA_CHUNKS = 6
NORM_EPS = 1e-6
GN_EPS = 1e-5
NEG_INF = -1e30
LOG2E = 1.4426950408889634

VMEM_LIMIT_BYTES = 56 * 1024 * 1024
LANES = 128


def _dot(a, b):
    return jnp.dot(a, b, preferred_element_type=F32)


def _dot_nt(a, b):
    return lax.dot_general(a, b, (((1,), (1,)), ((), ())), preferred_element_type=F32)


def _pad_rows(a, mult):
    r = (-a.shape[0]) % mult
    return a if r == 0 else jnp.concatenate([a, jnp.zeros((r, a.shape[1]), a.dtype)], axis=0)


def _modulated_norm(x, g, shift, scale):
    ms = jnp.mean(x * x, axis=-1, keepdims=True)
    return x * lax.rsqrt(ms + NORM_EPS) * (g * (1.0 + scale)) + shift


def _silu(x):
    return x * (1.0 / (1.0 + jnp.exp2(x * -LOG2E)))


def _resident(shape):
    nd = len(shape)
    return pl.BlockSpec(shape, lambda *_: (0,) * nd, pipeline_mode=pl.Buffered(1))


def _params(sem):
    return pltpu.CompilerParams(dimension_semantics=sem, vmem_limit_bytes=VMEM_LIMIT_BYTES)


def _ada_kernel(c_ref, w_ref, b_ref, o_ref):
    c = c_ref[...]
    a = _silu(c).astype(BF16)
    o_ref[...] = _dot(a, w_ref[...].astype(BF16)) + b_ref[...]


def _ada_modulation(c_all, ada_w, ada_b):
    nb = c_all.shape[0]
    width = ADA_CHUNKS * D_MODEL
    tn = 1536
    return pl.pallas_call(
        _ada_kernel,
        grid=(DEPTH, width // tn),
        in_specs=[
            pl.BlockSpec((nb, D_MODEL), lambda l, j: (0, 0)),
            pl.BlockSpec((None, D_MODEL, tn), lambda l, j: (l, 0, j)),
            pl.BlockSpec((None, 1, tn), lambda l, j: (l, 0, j)),
        ],
        out_specs=pl.BlockSpec((None, nb, tn), lambda l, j: (l, 0, j)),
        out_shape=jax.ShapeDtypeStruct((DEPTH, nb, width), F32),
        compiler_params=_params(("arbitrary", "arbitrary")),
        name="ada_modulation",
    )(c_all, ada_w, ada_b.reshape(DEPTH, 1, width))


def _mlp_kernel(x_ref, mod_ref, g_ref, w1_ref, w2_ref, o_ref, *, th):
    x = x_ref[...]
    h = _modulated_norm(x, g_ref[...], mod_ref[3], mod_ref[4]).astype(BF16)
    acc = jnp.zeros(x.shape, F32)
    for c in range(MLP_HIDDEN // th):
        a = _dot(h, w1_ref[:, c * th:(c + 1) * th])
        a = jnp.maximum(a, 0.0)
        a = a * a
        acc = acc + _dot(a.astype(BF16), w2_ref[c * th:(c + 1) * th, :])
    o_ref[...] = x + mod_ref[5] * acc


def _mlp_layer(x, mod, g, w1_all, w2_all, layer, tm):
    B, T, D = x.shape
    return pl.pallas_call(
        functools.partial(_mlp_kernel, th=512),
        grid=(B, T // tm),
        in_specs=[
            pl.BlockSpec((None, tm, D), lambda b, t: (b, t, 0)),
            pl.BlockSpec((None, ADA_CHUNKS, mod.shape[2], D), lambda b, t: (b, 0, 0, 0)),
            _resident((1, D)),
            pl.BlockSpec((None, D, MLP_HIDDEN), lambda b, t: (layer, 0, 0), pipeline_mode=pl.Buffered(1)),
            pl.BlockSpec((None, MLP_HIDDEN, D), lambda b, t: (layer, 0, 0), pipeline_mode=pl.Buffered(1)),
        ],
        out_specs=pl.BlockSpec((None, tm, D), lambda b, t: (b, t, 0)),
        out_shape=jax.ShapeDtypeStruct((B, T, D), F32),
        compiler_params=_params(("arbitrary", "arbitrary")),
        name="mlp_layer",
    )(x, mod, g, w1_all, w2_all)


def _chunk_cumsum(a, L):
    row = lax.broadcasted_iota(jnp.int32, a.shape, 0) % L
    s = 1
    while s < L:
        a = a + jnp.where(row >= s, pltpu.roll(a, s, axis=0), 0.0)
        s *= 2
    return a


def _gla_kernel(*refs, L, chained):
    if chained:
        (x_ref, mod_ref, ng_ref, wqk_ref, wv_ref, wg1_ref, wg2_ref, bg_ref, wr_ref, gn_ref, wo_ref,
         y_ref, s_ref) = refs

        @pl.when(pl.program_id(1) == 0)
        def _():
            s_ref[...] = jnp.zeros(s_ref.shape, F32)
    else:
        (x_ref, mod_ref, ng_ref, wqk_ref, wv_ref, wg1_ref, wg2_ref, bg_ref, wr_ref, gn_ref, wo_ref,
         s0_ref, y_ref, s_ref) = refs

    x = x_ref[...]
    tm = x.shape[0]
    nc = tm // L
    hb = _modulated_norm(x, ng_ref[...], mod_ref[0], mod_ref[1]).astype(BF16)
    g1 = _dot(hb, wg1_ref[...]).astype(BF16)
    rb = min(tm, GLA_RB)
    ncb = rb // L
    ri = lax.broadcasted_iota(jnp.int32, (rb, rb), 0)
    ci = lax.broadcasted_iota(jnp.int32, (rb, rb), 1)
    mask = ri >= ci
    if ncb > 1:
        mask = mask & (ci >= (ri // L) * L)
    rbp = rb + (-rb) % LANES
    lane_chunk = lax.broadcasted_iota(jnp.int32, (GLA_DK, rbp), 1) // L
    gn = gn_ref[...]

    o_heads = []
    for h0 in range(0, GLA_HEADS, GLA_LOCKSTEP):
        hs = list(range(h0, min(h0 + GLA_LOCKSTEP, GLA_HEADS)))
        kcs = [slice(h * GLA_DK, (h + 1) * GLA_DK) for h in hs]
        vcs = [slice(h * GLA_DV, (h + 1) * GLA_DV) for h in hs]
        zs = [_dot(g1, wg2_ref[:, kc]) + bg_ref[:, kc] for kc in kcs]
        loggs = [(jnp.minimum(z, 0.0) - jnp.log(1.0 + jnp.exp(-jnp.abs(z)))) * (1.0 / GLA_TAU) for z in zs]
        bs = [_chunk_cumsum(lg, L) for lg in loggs]
        ebs = [jnp.exp(b) for b in bs]
        enbs = [jnp.exp(-b) for b in bs]
        ebls = [[jnp.exp(b[c * L + L - 1:c * L + L, :]) for c in range(nc)] for b in bs]
        ebl_rows = [[jnp.broadcast_to(e, (L, GLA_DK)) for e in ebl] for ebl in ebls]
        ebl_rows = [r[0] if nc == 1 else jnp.concatenate(r, axis=0) for r in ebl_rows]
        qks = [_dot(hb, wqk_ref[:, 2 * h * GLA_DK:2 * (h + 1) * GLA_DK]) for h in hs]
        qds = [(qk[:, :GLA_DK] * (eb * GLA_DK ** -0.5)).astype(BF16) for qk, eb in zip(qks, ebs)]
        kdfs = [qk[:, GLA_DK:] * enb for qk, enb in zip(qks, enbs)]
        kds = [kdf.astype(BF16) for kdf in kdfs]
        kdecs = [kdf * r for kdf, r in zip(kdfs, ebl_rows)]
        vs = [_dot(hb, wv_ref[:, vc]).astype(BF16) for vc in vcs]
        gates = [_silu(_dot(hb, wr_ref[:, vc])) for vc in vcs]
        Ss = [s_ref[h] for h in hs] if chained else None
        o_blocks = [[] for _ in hs]
        blocks = list(range(0, tm, rb))
        vhs_b = [[_pad_rows(v[r0:r0 + rb], LANES) for v in vs] for r0 in blocks]
        atts_b = [[jnp.where(mask, _dot_nt(qd[r0:r0 + rb], kd[r0:r0 + rb]), 0.0).astype(BF16)
                   for qd, kd in zip(qds, kds)] for r0 in blocks]
        os_b = [[_dot(att, vh[:rb]) for att, vh in zip(atts, vhs)] for atts, vhs in zip(atts_b, vhs_b)]
        us_b = []
        for r0, vhs in zip(blocks, vhs_b):
            kdec_ts = [_pad_rows(kdec[r0:r0 + rb], LANES).T for kdec in kdecs]
            lhss = [[jnp.where(lane_chunk == c, kdec_t, 0.0) for c in range(ncb)] for kdec_t in kdec_ts]
            lhss = [(l[0] if ncb == 1 else jnp.concatenate(l, axis=0)).astype(BF16) for l in lhss]
            us_b.append([_dot(lhs, vh) for lhs, vh in zip(lhss, vhs)])
        for r0, os_, us in zip(blocks, os_b, us_b):
            o_inters = [[] for _ in hs]
            for c in range(ncb):
                if not chained:
                    Ss = [s0_ref[r0 // L + c, h] for h in hs]
                for i in range(len(hs)):
                    o_inters[i].append(_dot(qds[i][r0 + c * L:r0 + (c + 1) * L], Ss[i].astype(BF16)))
                dcols = [jnp.broadcast_to(ebl[r0 // L + c], (GLA_DK, GLA_DK)).T for ebl in ebls]
                Ss = [S * jnp.concatenate([dcol, dcol], axis=1) + u[c * GLA_DK:(c + 1) * GLA_DK]
                      for S, dcol, u in zip(Ss, dcols, us)]
                if not chained:
                    for i, h in enumerate(hs):
                        s_ref[r0 // L + c, h] = Ss[i]
            for i in range(len(hs)):
                oi = o_inters[i]
                o_blocks[i].append(os_[i] + (oi[0] if ncb == 1 else jnp.concatenate(oi, axis=0)))
        for i, h in enumerate(hs):
            if chained:
                s_ref[h] = Ss[i]
            o = o_blocks[i][0] if len(o_blocks[i]) == 1 else jnp.concatenate(o_blocks[i], axis=0)
            o = o * lax.rsqrt(jnp.mean(o * o, axis=-1, keepdims=True) + NORM_EPS) * gn
            o_heads.append((o * gates[i]).astype(BF16))
    y_ref[...] = x + mod_ref[2] * _dot(jnp.concatenate(o_heads, axis=1), wo_ref[...])


def _gla_layer(x, mod, ng, w, s0, tm, L):
    B, T, D = x.shape
    chained = s0 is None
    weights = [ng] + list(w)
    in_specs = [
        pl.BlockSpec((None, tm, D), lambda b, t: (b, t, 0)),
        pl.BlockSpec((None, ADA_CHUNKS, mod.shape[2], D), lambda b, t: (b, 0, 0, 0)),
    ] + [_resident(a.shape) for a in weights]
    args = [x, mod] + weights
    if chained:
        state_shape = (B, GLA_HEADS, GLA_DK, GLA_DV)
        state_spec = pl.BlockSpec((None,) + state_shape[1:], lambda b, t: (b, 0, 0, 0))
    else:
        assert B == 1 and tm == T and s0.shape[0] * L == T
        state_shape = s0.shape
        state_spec = pl.BlockSpec(state_shape, lambda b, t: (0, 0, 0, 0))
        in_specs.append(state_spec)
        args.append(s0)
    return pl.pallas_call(
        functools.partial(_gla_kernel, L=L, chained=chained),
        grid=(B, T // tm),
        in_specs=in_specs,
        out_specs=[pl.BlockSpec((None, tm, D), lambda b, t: (b, t, 0)), state_spec],
        out_shape=[jax.ShapeDtypeStruct((B, T, D), F32), jax.ShapeDtypeStruct(state_shape, F32)],
        compiler_params=_params(("arbitrary", "arbitrary")),
        name="gla_layer",
    )(*args)


def _ret_heads_chained(qb, kb, k, vb, s_ref, dintra_ref, dquery_ref, dkey_ref, log_decay, L):
    tm = qb.shape[0]
    o_heads = []
    for h0 in range(0, RET_HEADS, RET_LOCKSTEP):
        hs = list(range(h0, min(h0 + RET_LOCKSTEP, RET_HEADS)))
        kcs = [slice(h * RET_DK, (h + 1) * RET_DK) for h in hs]
        Ss = [s_ref[h] for h in hs]
        o_blocks = [[] for _ in hs]
        for r0 in range(0, tm, L):
            rows = slice(r0, r0 + L)
            vhs = [vb[r0:r0 + vb.shape[0] - tm + L, h * RET_DV:(h + 1) * RET_DV] for h in hs]
            atts = [(_dot_nt(qb[rows, kc], kb[rows, kc]) * dintra_ref[h]).astype(BF16) for h, kc in zip(hs, kcs)]
            os_ = [_dot(att, vh[:L]) + _dot(qb[rows, kc], S.astype(BF16))
                   * jnp.concatenate([dquery_ref[h]] * (RET_DV // LANES), axis=1)
                   for att, vh, kc, S, h in zip(atts, vhs, kcs, Ss, hs)]
            k_decs = [_pad_rows(k[rows, kc] * jnp.concatenate([dkey_ref[h]] * (RET_DK // LANES), axis=1), LANES)
                      for h, kc in zip(hs, kcs)]
            Ss = [S * math.exp(log_decay[h] * L) + _dot(k_dec.T.astype(BF16), vh)
                  for h, S, k_dec, vh in zip(hs, Ss, k_decs, vhs)]
            ocs = [o - jnp.mean(o, axis=-1, keepdims=True) for o in os_]
            for i, oc in enumerate(ocs):
                o_blocks[i].append(oc * lax.rsqrt(jnp.mean(oc * oc, axis=-1, keepdims=True) + GN_EPS))
        for i, h in enumerate(hs):
            s_ref[h] = Ss[i]
            o_heads.append(o_blocks[i][0] if len(o_blocks[i]) == 1 else jnp.concatenate(o_blocks[i], axis=0))
    return o_heads


def _ret_heads_separate(qb, kb, k, vb, s0_ref, s_ref, dintra_ref, dquery_ref, dkey_ref, log_decay, L):
    tm = qb.shape[0]
    o_heads = []
    for h0 in range(0, RET_HEADS, RET_LOCKSTEP):
        hs = list(range(h0, min(h0 + RET_LOCKSTEP, RET_HEADS)))
        kcs = [slice(h * RET_DK, (h + 1) * RET_DK) for h in hs]
        vcs = [slice(h * RET_DV, (h + 1) * RET_DV) for h in hs]
        dqs = [jnp.concatenate([dquery_ref[h]] * (RET_DV // LANES), axis=1) for h in hs]
        dks = [jnp.concatenate([dkey_ref[h]] * (RET_DK // LANES), axis=1) for h in hs]
        atts = [(_dot_nt(qb[:, kc], kb[:, kc]) * dintra_ref[h]).astype(BF16) for h, kc in zip(hs, kcs)]
        k_decs = [k[:, kc] * dk for kc, dk in zip(kcs, dks)]
        o_inters = [[] for _ in hs]
        for c in range(tm // L):
            crow = slice(c * L, (c + 1) * L)
            Ss = [s0_ref[c, h] for h in hs]
            for i in range(len(hs)):
                o_inters[i].append(_dot(qb[crow, kcs[i]], Ss[i].astype(BF16)) * dqs[i][crow])
            for h, S, k_dec, vc in zip(hs, Ss, k_decs, vcs):
                s_ref[c, h] = (S * math.exp(log_decay[h] * L)
                               + _dot(_pad_rows(k_dec[crow], LANES).T.astype(BF16), _pad_rows(vb[crow, vc], LANES)))
        os_ = [_dot(att, vb[:, vc]) + (oi[0] if len(oi) == 1 else jnp.concatenate(oi, axis=0))
               for att, vc, oi in zip(atts, vcs, o_inters)]
        ocs = [o - jnp.mean(o, axis=-1, keepdims=True) for o in os_]
        o_heads += [oc * lax.rsqrt(jnp.mean(oc * oc, axis=-1, keepdims=True) + GN_EPS) for oc in ocs]
    return o_heads


def _ret_kernel(*refs, L, chained):
    if chained:
        (x_ref, mod_ref, ng_ref, wq_ref, wk_ref, wv_ref, wg_ref, wo_ref, y_ref, s_ref,
         dintra_ref, dquery_ref, dkey_ref) = refs
    else:
        (x_ref, mod_ref, ng_ref, wq_ref, wk_ref, wv_ref, wg_ref, wo_ref, s0_ref, y_ref, s_ref,
         dintra_ref, dquery_ref, dkey_ref) = refs
    rb = dintra_ref.shape[1]
    log_decay = [math.log(1.0 - 2.0 ** (-5.0 - h)) for h in range(RET_HEADS)]

    @pl.when((pl.program_id(0) == 0) & (pl.program_id(1) == 0))
    def _():
        ri = lax.broadcasted_iota(jnp.int32, (rb, rb), 0)
        ci = lax.broadcasted_iota(jnp.int32, (rb, rb), 1)
        rel = (ri - ci).astype(F32)
        keep = (ri >= ci) & (ci >= (ri // L) * L)
        pos = (lax.broadcasted_iota(jnp.int32, (rb, LANES), 0) % L).astype(F32)
        for h in range(RET_HEADS):
            dintra_ref[h] = jnp.where(keep, jnp.exp(log_decay[h] * jnp.maximum(rel, 0.0)), 0.0)
            dquery_ref[h] = jnp.exp(log_decay[h] * (pos + 1.0))
            dkey_ref[h] = jnp.exp(log_decay[h] * (L - 1.0 - pos))

    if chained:
        @pl.when(pl.program_id(1) == 0)
        def _():
            s_ref[...] = jnp.zeros(s_ref.shape, F32)

    x = x_ref[...]
    hb = _modulated_norm(x, ng_ref[...], mod_ref[0], mod_ref[1]).astype(BF16)
    q = _dot(hb, wq_ref[...])
    k = _dot(hb, wk_ref[...]) * (RET_DK ** -0.5)
    v = _dot(hb, wv_ref[...])
    g = _dot(hb, wg_ref[...])
    qb = q.astype(BF16)
    kb = k.astype(BF16)
    if chained:
        o_heads = _ret_heads_chained(qb, kb, k, _pad_rows(v.astype(BF16), LANES), s_ref,
                                     dintra_ref, dquery_ref, dkey_ref, log_decay, L)
    else:
        o_heads = _ret_heads_separate(qb, kb, k, v.astype(BF16), s0_ref, s_ref,
                                      dintra_ref, dquery_ref, dkey_ref, log_decay, L)
    o = jnp.concatenate(o_heads, axis=1)
    o = (o * _silu(g)).astype(BF16)
    y_ref[...] = x + mod_ref[2] * _dot(o, wo_ref[...])


def _ret_layer(x, mod, ng, w, s0, tm, L):
    B, T, D = x.shape
    chained = s0 is None
    rb = L if chained else tm
    weights = [ng] + list(w)
    in_specs = [
        pl.BlockSpec((None, tm, D), lambda b, t: (b, t, 0)),
        pl.BlockSpec((None, ADA_CHUNKS, mod.shape[2], D), lambda b, t: (b, 0, 0, 0)),
    ] + [_resident(a.shape) for a in weights]
    args = [x, mod] + weights
    if chained:
        assert tm % L == 0
        state_shape = (B, RET_HEADS, RET_DK, RET_DV)
        state_spec = pl.BlockSpec((None,) + state_shape[1:], lambda b, t: (b, 0, 0, 0))
    else:
        assert tm == T and T % L == 0 and s0.shape[0] * L == B * T
        state_shape = s0.shape
        state_spec = pl.BlockSpec((T // L,) + state_shape[1:], lambda b, t: (b, 0, 0, 0))
        in_specs.append(state_spec)
        args.append(s0)
    return pl.pallas_call(
        functools.partial(_ret_kernel, L=L, chained=chained),
        grid=(B, T // tm),
        in_specs=in_specs,
        out_specs=[pl.BlockSpec((None, tm, D), lambda b, t: (b, t, 0)), state_spec],
        out_shape=[jax.ShapeDtypeStruct((B, T, D), F32), jax.ShapeDtypeStruct(state_shape, F32)],
        scratch_shapes=[pltpu.VMEM((RET_HEADS, rb, rb), F32), pltpu.VMEM((RET_HEADS, rb, LANES), F32),
                        pltpu.VMEM((RET_HEADS, rb, LANES), F32)],
        compiler_params=_params(("arbitrary", "arbitrary")),
        name="ret_layer",
    )(*args)


def _head_mean_square(a, hsum):
    w = hsum.shape[0]
    parts = [_dot((a[:, j:j + w] * a[:, j:j + w]).astype(BF16), hsum) for j in range(0, a.shape[1], w)]
    return parts[0] if len(parts) == 1 else jnp.concatenate(parts, axis=1)


def _tile_head_lanes(a, kv):
    t = a[:, (kv // 2) * LANES:(kv // 2 + 1) * LANES]
    low = lax.broadcasted_iota(jnp.int32, t.shape, 1) < SWA_HD
    keep = low if kv % 2 == 0 else jnp.logical_not(low)
    return jnp.where(keep, t, pltpu.roll(t, SWA_HD, axis=1)).astype(BF16)


def _swa_kernel(*refs, Lq, is_prompt):
    if is_prompt:
        (sink_ref, x_ref, mod_ref, ng_ref, wqkv_ref, qn_ref, kn_ref, wo_ref, y_ref, kc_ref, vc_ref,
         bias_ref) = refs
    else:
        (sink_ref, x_ref, mod_ref, ng_ref, wqkv_ref, qn_ref, kn_ref, wo_ref, ck_ref, cv_ref,
         y_ref, kc_ref, vc_ref, bias_ref) = refs
    t = pl.program_id(1)
    x = x_ref[...]
    tm = x.shape[0]
    nq = SWA_Q_HEADS * SWA_HD
    nk = SWA_KV_HEADS * SWA_HD
    gw = SWA_GROUP * SWA_HD
    nkeys = WINDOW + Lq
    nkp = nkeys + 1 + (-(nkeys + 1)) % LANES
    rows_q = SWA_GROUP * Lq

    @pl.when((pl.program_id(0) == 0) & (t == 0))
    def _():
        ri = lax.broadcasted_iota(jnp.int32, (rows_q, nkp), 0)
        ji = lax.broadcasted_iota(jnp.int32, (rows_q, nkp), 1)
        grp = ri // Lq
        dist = jnp.abs((ri - grp * Lq) - ji + WINDOW).astype(F32)
        for kv in range(SWA_KV_HEADS):
            sink = jnp.zeros((rows_q, nkp), F32)
            for gi in range(SWA_GROUP):
                sink = jnp.where(grp == gi, sink_ref[kv * SWA_GROUP + gi] * LOG2E, sink)
            bias = -LOG2E * jnp.exp2(-0.5 * (kv * SWA_GROUP + grp + 1).astype(F32)) * dist
            bias_ref[kv] = jnp.where(ji < nkeys, bias, jnp.where(ji == nkeys, sink, NEG_INF))

    hb = _modulated_norm(x, ng_ref[...], mod_ref[0], mod_ref[1]).astype(BF16)
    hi = lax.broadcasted_iota(jnp.int32, (gw, gw), 0) // SWA_HD
    hj = lax.broadcasted_iota(jnp.int32, (gw, gw), 1) // SWA_HD
    hsum = jnp.where(hi == hj, 1.0 / SWA_HD, 0.0).astype(BF16)
    kv_new = _dot(hb, wqkv_ref[:, nq:])
    k = kv_new[:, :nk]
    v_new = kv_new[:, nk:]
    k_new = k * lax.rsqrt(_head_mean_square(k, hsum) + NORM_EPS) * kn_ref[...]
    q_gain = qn_ref[...] * (SWA_HD ** -0.5 * LOG2E)
    qs_kv = []
    for kv0 in range(0, SWA_KV_HEADS, 2):
        qg = _dot(hb, wqkv_ref[:, kv0 * gw:(kv0 + 2) * gw])
        qg = (qg * lax.rsqrt(_head_mean_square(qg, hsum) + NORM_EPS) * q_gain[:, kv0 * gw:(kv0 + 2) * gw]).astype(BF16)
        qs_kv += [qg[:, :gw], qg[:, gw:]]

    nchunk = tm // Lq
    k_pad = jnp.zeros((nkp - nkeys, nk), F32)
    if is_prompt:
        @pl.when(t == 0)
        def _():
            kc_ref[...] = jnp.zeros(kc_ref.shape, F32)
            vc_ref[...] = jnp.zeros(vc_ref.shape, F32)
        k_exts = [jnp.concatenate([kc_ref[...], k_new, k_pad], axis=0)]
        v_exts = [jnp.concatenate([vc_ref[...], v_new], axis=0)]
        src = [(0, c * Lq) for c in range(nchunk)]
    else:
        k_exts = [jnp.concatenate([ck_ref[c], k_new[c * Lq:(c + 1) * Lq], k_pad], axis=0) for c in range(nchunk)]
        v_exts = [jnp.concatenate([cv_ref[c], v_new[c * Lq:(c + 1) * Lq]], axis=0) for c in range(nchunk)]
        src = [(c, 0) for c in range(nchunk)]

    low_half = lax.broadcasted_iota(jnp.int32, (Lq, LANES), 1) < SWA_HD
    ji = lax.broadcasted_iota(jnp.int32, (rows_q, nkp), 1)
    is_key = ji < nkeys
    zero = jnp.zeros((), BF16)

    o_cols = []
    for kv in range(SWA_KV_HEADS):
        kts = [_tile_head_lanes(k_ext, kv) for k_ext in k_exts]
        vts = [_tile_head_lanes(v_ext, kv) for v_ext in v_exts]
        bias = bias_ref[kv]
        o_rows = []
        for c0 in range(0, nchunk, SWA_LOCKSTEP):
            cs = list(range(c0, min(c0 + SWA_LOCKSTEP, nchunk)))
            r0s = [c * Lq for c in cs]
            kbs = [kts[src[c][0]][src[c][1]:src[c][1] + nkp] for c in cs]
            vbs = [vts[src[c][0]][src[c][1]:src[c][1] + nkeys] for c in cs]
            qss = []
            for r0 in r0s:
                qs = []
                for gi in range(SWA_GROUP):
                    qt = qs_kv[kv][r0:r0 + Lq, (gi // 2) * LANES:(gi // 2 + 1) * LANES]
                    qs.append(jnp.where(low_half if gi % 2 == 0 else jnp.logical_not(low_half), qt, zero))
                qss.append(jnp.concatenate(qs, axis=0))
            logits = [jnp.where(is_key, _dot_nt(qs, kb), 0.0) + bias for qs, kb in zip(qss, kbs)]
            if is_prompt:
                logits = [jnp.where(ji >= WINDOW - r0 - t * tm, lg, NEG_INF) if r0 < WINDOW else lg
                          for lg, r0 in zip(logits, r0s)]
            ps = [jnp.exp2(lg - jnp.max(lg, axis=-1, keepdims=True)) for lg in logits]
            dens = [jnp.sum(p, axis=-1, keepdims=True) for p in ps]
            ocs = [_dot(p[:, :nkeys].astype(BF16), vb) * (1.0 / den) for p, vb, den in zip(ps, vbs, dens)]
            for oc in ocs:
                o_rows.append(jnp.concatenate(
                    [jnp.where(low_half, oc[2 * j * Lq:(2 * j + 1) * Lq], oc[(2 * j + 1) * Lq:(2 * j + 2) * Lq])
                     for j in range(SWA_GROUP // 2)], axis=1))
        o_cols.append(o_rows[0] if len(o_rows) == 1 else jnp.concatenate(o_rows, axis=0))
    o = jnp.concatenate(o_cols, axis=1)
    y_ref[...] = x + mod_ref[2] * _dot(o.astype(BF16), wo_ref[...])
    if is_prompt:
        kc_ref[...] = k_new[tm - WINDOW:]
        vc_ref[...] = v_new[tm - WINDOW:]
    else:
        kc_ref[...] = k_new
        vc_ref[...] = v_new


def _swa_layer(x, mod, ng, w, cache, tm, Lq):
    B, T, D = x.shape
    is_prompt = cache is None
    wqkv, qn, kn, sinks, wo = w
    nk = SWA_KV_HEADS * SWA_HD
    in_specs = [
        pl.BlockSpec(memory_space=pltpu.SMEM),
        pl.BlockSpec((None, tm, D), lambda b, t: (b, t, 0)),
        pl.BlockSpec((None, ADA_CHUNKS, mod.shape[2], D), lambda b, t: (b, 0, 0, 0)),
        _resident((1, D)),
        _resident(wqkv.shape), _resident(qn.shape), _resident(kn.shape), _resident(wo.shape),
    ]
    args = [sinks, x, mod, ng, wqkv, qn, kn, wo]
    if is_prompt:
        rows_out = WINDOW
    else:
        ck, cv = cache
        assert B == 1 and tm == T and ck.shape[0] * Lq == T
        in_specs += [pl.BlockSpec(ck.shape, lambda b, t: (0, 0, 0))] * 2
        args += [ck, cv]
        rows_out = T
    cache_spec = pl.BlockSpec((None, rows_out, nk), lambda b, t: (b, 0, 0))
    nkp = WINDOW + Lq + 1 + (-(WINDOW + Lq + 1)) % LANES
    return pl.pallas_call(
        functools.partial(_swa_kernel, Lq=Lq, is_prompt=is_prompt),
        grid=(B, T // tm),
        in_specs=in_specs,
        out_specs=[pl.BlockSpec((None, tm, D), lambda b, t: (b, t, 0)), cache_spec, cache_spec],
        out_shape=[jax.ShapeDtypeStruct((B, T, D), F32),
                   jax.ShapeDtypeStruct((B, rows_out, nk), F32),
                   jax.ShapeDtypeStruct((B, rows_out, nk), F32)],
        scratch_shapes=[pltpu.VMEM((SWA_KV_HEADS, SWA_GROUP * Lq, nkp), F32)],
        compiler_params=_params(("arbitrary", "arbitrary")),
        name="swa_layer",
    )(*args)


def _tile(T, pref):
    return pref if T % pref == 0 else T


def kernel(x_prompt, x_sample, state_gla_l0, state_ret_l1, cache_swa_k_l2, cache_swa_v_l2, state_gla_l3, c_prompt, c_sample, norm_mix, norm_mlp, ada_w, ada_b, mlp_w1, mlp_w2, gla_wq, gla_wk, gla_wv, gla_wg1, gla_wg2, gla_bg, gla_wr, gla_norm, gla_wo, ret_wq, ret_wk, ret_wv, ret_wg, ret_wo, swa_wqkv, swa_qnorm, swa_knorm, swa_sinks, swa_wo):
    bp = x_prompt.shape[0]
    bs = x_sample.shape[0]
    Tp = x_prompt.shape[1]
    Ts = x_sample.shape[1]
    gla_states = [state_gla_l0, state_gla_l3]
    ret_states = [state_ret_l1]
    swa_caches = [(cache_swa_k_l2, cache_swa_v_l2)]

    mod_all = _ada_modulation(jnp.concatenate([c_prompt, c_sample], axis=0), ada_w, ada_b)
    mod_all = mod_all.reshape(DEPTH, bp + bs, ADA_CHUNKS, D_MODEL)

    w1_all = mlp_w1.astype(BF16)
    w2_all = mlp_w2.astype(BF16)
    yp, ys = x_prompt, x_sample
    new_state = []
    for layer in range(DEPTH):
        kind, j = layer % N_MIXERS, layer // N_MIXERS
        mod_p = mod_all[layer, :bp]
        mod_s = mod_all[layer, bp:]
        mod_rows = jnp.repeat(mod_s.transpose(1, 0, 2), Ts, axis=1)[None]
        ng = norm_mix[layer].reshape(1, D_MODEL)
        if kind == 0:
            wqk = jnp.concatenate([gla_wq[j].reshape(D_MODEL, GLA_HEADS, GLA_DK),
                                   gla_wk[j].reshape(D_MODEL, GLA_HEADS, GLA_DK)], axis=2).reshape(D_MODEL, -1)
            w = (wqk.astype(BF16), gla_wv[j].astype(BF16), gla_wg1[j].astype(BF16),
                 gla_wg2[j].astype(BF16), gla_bg[j].reshape(1, -1), gla_wr[j].astype(BF16),
                 gla_norm[j].reshape(1, -1), gla_wo[j].astype(BF16))
            yp, st_p = _gla_layer(yp, mod_p[:, :, None, :], ng, w, None, _tile(Tp, 1024), min(CHUNK, Tp))
            assert Ts <= CHUNK
            ys, st_s = _gla_layer(ys.reshape(1, bs * Ts, D_MODEL), mod_rows, ng, w, gla_states[j], bs * Ts, Ts)
            ys = ys.reshape(bs, Ts, D_MODEL)
            new_state += [st_p, st_s]
        elif kind == 1:
            w = (ret_wq[j].astype(BF16), ret_wk[j].astype(BF16), ret_wv[j].astype(BF16), ret_wg[j].astype(BF16),
                 ret_wo[j].astype(BF16))
            yp, st_p = _ret_layer(yp, mod_p[:, :, None, :], ng, w, None, _tile(Tp, 512), min(RET_RB, Tp))
            assert Ts <= CHUNK
            gs = RET_SAMPLE_GROUP if bs % RET_SAMPLE_GROUP == 0 else bs
            mod_g = jnp.repeat(mod_s.reshape(bs // gs, gs, ADA_CHUNKS, D_MODEL).transpose(0, 2, 1, 3), Ts, axis=2)
            ys, st_s = _ret_layer(ys.reshape(bs // gs, gs * Ts, D_MODEL), mod_g, ng, w, ret_states[j], gs * Ts, Ts)
            ys = ys.reshape(bs, Ts, D_MODEL)
            new_state += [st_p, st_s]
        else:
            w = (swa_wqkv[j].astype(BF16), jnp.tile(swa_qnorm[j], SWA_Q_HEADS).reshape(1, -1),
                 jnp.tile(swa_knorm[j], SWA_KV_HEADS).reshape(1, -1), swa_sinks[j], swa_wo[j].astype(BF16))
            yp, k_p, v_p = _swa_layer(yp, mod_p[:, :, None, :], ng, w, None, _tile(Tp, 1024), CHUNK)
            assert Ts <= CHUNK
            nk = SWA_KV_HEADS * SWA_HD
            ck, cv = (c.reshape(bs, WINDOW, nk) for c in swa_caches[j])
            ys, k_s, v_s = _swa_layer(ys.reshape(1, bs * Ts, D_MODEL), mod_rows, ng, w, (ck, cv), bs * Ts, Ts)
            ys = ys.reshape(bs, Ts, D_MODEL)
            new_state += [a.reshape(bp, WINDOW, SWA_KV_HEADS, SWA_HD) for a in (k_p, v_p)]
            new_state += [a.reshape(bs, Ts, SWA_KV_HEADS, SWA_HD) for a in (k_s, v_s)]
        gm = norm_mlp[layer].reshape(1, D_MODEL)
        yp = _mlp_layer(yp, mod_p[:, :, None, :], gm, w1_all, w2_all, layer, _tile(Tp, 1024))
        ys = _mlp_layer(ys.reshape(1, bs * Ts, D_MODEL), mod_rows, gm, w1_all, w2_all, layer,
                        bs * Ts).reshape(bs, Ts, D_MODEL)
    return (yp, ys, *new_state)
```

```python
import functools
import math

import jax
import jax.numpy as jnp
from jax import lax
from jax.experimental import pallas as pl
from jax.experimental.pallas import tpu as pltpu

F32 = jnp.float32
BF16 = jnp.bfloat16

D_MODEL = 1024
DEPTH = 4
CHUNK = 64
N_MIXERS = 3
GLA_HEADS = 4
GLA_DK = 128
GLA_DV = 256
GLA_RANK = 16
GLA_TAU = 16.0
GLA_RB = 256
GLA_LOCKSTEP = 2
RET_HEADS = 4
RET_DK = 256
RET_DV = 512
RET_RB = 256
RET_LOCKSTEP = 2
RET_SAMPLE_GROUP = 4
SWA_Q_HEADS = 16
SWA_KV_HEADS = 4
SWA_GROUP = 4
SWA_HD = 64
WINDOW = 128
SWA_LOCKSTEP = 4
MLP_HIDDEN = 4 * D_MODEL
ADA_CHUNKS = 6
NORM_EPS = 1e-6
GN_EPS = 1e-5
NEG_INF = -1e30
LOG2E = 1.4426950408889634

VMEM_LIMIT_BYTES = 56 * 1024 * 1024
LANES = 128


def _dot(a, b):
    return jnp.dot(a, b, preferred_element_type=F32)


def _dot_nt(a, b):
    return lax.dot_general(a, b, (((1,), (1,)), ((), ())), preferred_element_type=F32)


def _pad_rows(a, mult):
    r = (-a.shape[0]) % mult
    return a if r == 0 else jnp.concatenate([a, jnp.zeros((r, a.shape[1]), a.dtype)], axis=0)


def _modulated_norm(x, g, shift, scale):
    ms = jnp.mean(x * x, axis=-1, keepdims=True)
    return x * lax.rsqrt(ms + NORM_EPS) * (g * (1.0 + scale)) + shift


def _silu(x):
    return x * (1.0 / (1.0 + jnp.exp2(x * -LOG2E)))


def _resident(shape):
    nd = len(shape)
    return pl.BlockSpec(shape, lambda *_: (0,) * nd, pipeline_mode=pl.Buffered(1))


def _params(sem):
    return pltpu.CompilerParams(dimension_semantics=sem, vmem_limit_bytes=VMEM_LIMIT_BYTES)


def _ada_kernel(c_ref, w_ref, b_ref, o_ref):
    c = c_ref[...]
    a = _silu(c).astype(BF16)
    o_ref[...] = _dot(a, w_ref[...].astype(BF16)) + b_ref[...]


def _ada_modulation(c_all, ada_w, ada_b):
    nb = c_all.shape[0]
    width = ADA_CHUNKS * D_MODEL
    tn = 1536
    return pl.pallas_call(
        _ada_kernel,
        grid=(DEPTH, width // tn),
        in_specs=[
            pl.BlockSpec((nb, D_MODEL), lambda l, j: (0, 0)),
            pl.BlockSpec((None, D_MODEL, tn), lambda l, j: (l, 0, j)),
            pl.BlockSpec((None, 1, tn), lambda l, j: (l, 0, j)),
        ],
        out_specs=pl.BlockSpec((None, nb, tn), lambda l, j: (l, 0, j)),
        out_shape=jax.ShapeDtypeStruct((DEPTH, nb, width), F32),
        compiler_params=_params(("arbitrary", "arbitrary")),
        name="ada_modulation",
    )(c_all, ada_w, ada_b.reshape(DEPTH, 1, width))


def _mlp_kernel(x_ref, mod_ref, g_ref, w1_ref, w2_ref, o_ref, *, th):
    x = x_ref[...]
    h = _modulated_norm(x, g_ref[...], mod_ref[3], mod_ref[4]).astype(BF16)
    acc = jnp.zeros(x.shape, F32)
    for c in range(MLP_HIDDEN // th):
        a = _dot(h, w1_ref[:, c * th:(c + 1) * th])
        a = jnp.maximum(a, 0.0)
        a = a * a
        acc = acc + _dot(a.astype(BF16), w2_ref[c * th:(c + 1) * th, :])
    o_ref[...] = x + mod_ref[5] * acc


def _mlp_layer(x, mod, g, w1_all, w2_all, layer, tm):
    B, T, D = x.shape
    return pl.pallas_call(
        functools.partial(_mlp_kernel, th=512),
        grid=(B, T // tm),
        in_specs=[
            pl.BlockSpec((None, tm, D), lambda b, t: (b, t, 0)),
            pl.BlockSpec((None, ADA_CHUNKS, mod.shape[2], D), lambda b, t: (b, 0, 0, 0)),
            _resident((1, D)),
            pl.BlockSpec((None, D, MLP_HIDDEN), lambda b, t: (layer, 0, 0), pipeline_mode=pl.Buffered(1)),
            pl.BlockSpec((None, MLP_HIDDEN, D), lambda b, t: (layer, 0, 0), pipeline_mode=pl.Buffered(1)),
        ],
        out_specs=pl.BlockSpec((None, tm, D), lambda b, t: (b, t, 0)),
        out_shape=jax.ShapeDtypeStruct((B, T, D), F32),
        compiler_params=_params(("arbitrary", "arbitrary")),
        name="mlp_layer",
    )(x, mod, g, w1_all, w2_all)


def _chunk_cumsum(a, L):
    row = lax.broadcasted_iota(jnp.int32, a.shape, 0) % L
    s = 1
    while s < L:
        a = a + jnp.where(row >= s, pltpu.roll(a, s, axis=0), 0.0)
        s *= 2
    return a


def _gla_kernel(*refs, L, chained):
    if chained:
        (x_ref, mod_ref, ng_ref, wqk_ref, wv_ref, wg1_ref, wg2_ref, bg_ref, wr_ref, gn_ref, wo_ref,
         y_ref, s_ref) = refs

        @pl.when(pl.program_id(1) == 0)
        def _():
            s_ref[...] = jnp.zeros(s_ref.shape, F32)
    else:
        (x_ref, mod_ref, ng_ref, wqk_ref, wv_ref, wg1_ref, wg2_ref, bg_ref, wr_ref, gn_ref, wo_ref,
         s0_ref, y_ref, s_ref) = refs

    x = x_ref[...]
    tm = x.shape[0]
    nc = tm // L
    hb = _modulated_norm(x, ng_ref[...], mod_ref[0], mod_ref[1]).astype(BF16)
    g1 = _dot(hb, wg1_ref[...]).astype(BF16)
    rb = min(tm, GLA_RB)
    ncb = rb // L
    ri = lax.broadcasted_iota(jnp.int32, (rb, rb), 0)
    ci = lax.broadcasted_iota(jnp.int32, (rb, rb), 1)
    mask = ri >= ci
    if ncb > 1:
        mask = mask & (ci >= (ri // L) * L)
    rbp = rb + (-rb) % LANES
    lane_chunk = lax.broadcasted_iota(jnp.int32, (GLA_DK, rbp), 1) // L
    gn = gn_ref[...]

    o_heads = []
    for h0 in range(0, GLA_HEADS, GLA_LOCKSTEP):
        hs = list(range(h0, min(h0 + GLA_LOCKSTEP, GLA_HEADS)))
        kcs = [slice(h * GLA_DK, (h + 1) * GLA_DK) for h in hs]
        vcs = [slice(h * GLA_DV, (h + 1) * GLA_DV) for h in hs]
        zs = [_dot(g1, wg2_ref[:, kc]) + bg_ref[:, kc] for kc in kcs]
        loggs = [(jnp.minimum(z, 0.0) - jnp.log(1.0 + jnp.exp(-jnp.abs(z)))) * (1.0 / GLA_TAU) for z in zs]
        bs = [_chunk_cumsum(lg, L) for lg in loggs]
        ebs = [jnp.exp(b) for b in bs]
        enbs = [jnp.exp(-b) for b in bs]
        ebls = [[jnp.exp(b[c * L + L - 1:c * L + L, :]) for c in range(nc)] for b in bs]
        ebl_rows = [[jnp.broadcast_to(e, (L, GLA_DK)) for e in ebl] for ebl in ebls]
        ebl_rows = [r[0] if nc == 1 else jnp.concatenate(r, axis=0) for r in ebl_rows]
        qks = [_dot(hb, wqk_ref[:, 2 * h * GLA_DK:2 * (h + 1) * GLA_DK]) for h in hs]
        qds = [(qk[:, :GLA_DK] * (eb * GLA_DK ** -0.5)).astype(BF16) for qk, eb in zip(qks, ebs)]
        kdfs = [qk[:, GLA_DK:] * enb for qk, enb in zip(qks, enbs)]
        kds = [kdf.astype(BF16) for kdf in kdfs]
        kdecs = [kdf * r for kdf, r in zip(kdfs, ebl_rows)]
        vs = [_dot(hb, wv_ref[:, vc]).astype(BF16) for vc in vcs]
        gates = [_silu(_dot(hb, wr_ref[:, vc])) for vc in vcs]
        Ss = [s_ref[h] for h in hs] if chained else None
        o_blocks = [[] for _ in hs]
        blocks = list(range(0, tm, rb))
        vhs_b = [[_pad_rows(v[r0:r0 + rb], LANES) for v in vs] for r0 in blocks]
        atts_b = [[jnp.where(mask, _dot_nt(qd[r0:r0 + rb], kd[r0:r0 + rb]), 0.0).astype(BF16)
                   for qd, kd in zip(qds, kds)] for r0 in blocks]
        os_b = [[_dot(att, vh[:rb]) for att, vh in zip(atts, vhs)] for atts, vhs in zip(atts_b, vhs_b)]
        us_b = []
        for r0, vhs in zip(blocks, vhs_b):
            kdec_ts = [_pad_rows(kdec[r0:r0 + rb], LANES).T for kdec in kdecs]
            lhss = [[jnp.where(lane_chunk == c, kdec_t, 0.0) for c in range(ncb)] for kdec_t in kdec_ts]
            lhss = [(l[0] if ncb == 1 else jnp.concatenate(l, axis=0)).astype(BF16) for l in lhss]
            us_b.append([_dot(lhs, vh) for lhs, vh in zip(lhss, vhs)])
        for r0, os_, us in zip(blocks, os_b, us_b):
            o_inters = [[] for _ in hs]
            for c in range(ncb):
                if not chained:
                    Ss = [s0_ref[r0 // L + c, h] for h in hs]
                for i in range(len(hs)):
                    o_inters[i].append(_dot(qds[i][r0 + c * L:r0 + (c + 1) * L], Ss[i].astype(BF16)))
                dcols = [jnp.broadcast_to(ebl[r0 // L + c], (GLA_DK, GLA_DK)).T for ebl in ebls]
                Ss = [S * jnp.concatenate([dcol, dcol], axis=1) + u[c * GLA_DK:(c + 1) * GLA_DK]
                      for S, dcol, u in zip(Ss, dcols, us)]
                if not chained:
                    for i, h in enumerate(hs):
                        s_ref[r0 // L + c, h] = Ss[i]
            for i in range(len(hs)):
                oi = o_inters[i]
                o_blocks[i].append(os_[i] + (oi[0] if ncb == 1 else jnp.concatenate(oi, axis=0)))
        for i, h in enumerate(hs):
            if chained:
                s_ref[h] = Ss[i]
            o = o_blocks[i][0] if len(o_blocks[i]) == 1 else jnp.concatenate(o_blocks[i], axis=0)
            o = o * lax.rsqrt(jnp.mean(o * o, axis=-1, keepdims=True) + NORM_EPS) * gn
            o_heads.append((o * gates[i]).astype(BF16))
    y_ref[...] = x + mod_ref[2] * _dot(jnp.concatenate(o_heads, axis=1), wo_ref[...])


def _gla_layer(x, mod, ng, w, s0, tm, L):
    B, T, D = x.shape
    chained = s0 is None
    weights = [ng] + list(w)
    in_specs = [
        pl.BlockSpec((None, tm, D), lambda b, t: (b, t, 0)),
        pl.BlockSpec((None, ADA_CHUNKS, mod.shape[2], D), lambda b, t: (b, 0, 0, 0)),
    ] + [_resident(a.shape) for a in weights]
    args = [x, mod] + weights
    if chained:
        state_shape = (B, GLA_HEADS, GLA_DK, GLA_DV)
        state_spec = pl.BlockSpec((None,) + state_shape[1:], lambda b, t: (b, 0, 0, 0))
    else:
        assert B == 1 and tm == T and s0.shape[0] * L == T
        state_shape = s0.shape
        state_spec = pl.BlockSpec(state_shape, lambda b, t: (0, 0, 0, 0))
        in_specs.append(state_spec)
        args.append(s0)
    return pl.pallas_call(
        functools.partial(_gla_kernel, L=L, chained=chained),
        grid=(B, T // tm),
        in_specs=in_specs,
        out_specs=[pl.BlockSpec((None, tm, D), lambda b, t: (b, t, 0)), state_spec],
        out_shape=[jax.ShapeDtypeStruct((B, T, D), F32), jax.ShapeDtypeStruct(state_shape, F32)],
        compiler_params=_params(("arbitrary", "arbitrary")),
        name="gla_layer",
    )(*args)


def _ret_heads_chained(hb, wq_ref, wk_ref, wv_ref, wg_ref, s_ref, dintra_ref, dquery_ref, dkey_ref, log_decay, L):
    tm = hb.shape[0]
    o_heads = []
    for h0 in range(0, RET_HEADS, RET_LOCKSTEP):
        hs = list(range(h0, min(h0 + RET_LOCKSTEP, RET_HEADS)))
        nh = len(hs)
        kall = slice(hs[0] * RET_DK, (hs[-1] + 1) * RET_DK)
        vall = slice(hs[0] * RET_DV, (hs[-1] + 1) * RET_DV)
        qb = _dot(hb, wq_ref[:, kall]).astype(BF16)
        k = _dot(hb, wk_ref[:, kall]) * (RET_DK ** -0.5)
        kb = k.astype(BF16)
        vb = _pad_rows(_dot(hb, wv_ref[:, vall]).astype(BF16), LANES)
        gate = _silu(_dot(hb, wg_ref[:, vall]))
        kcs = [slice(i * RET_DK, (i + 1) * RET_DK) for i in range(nh)]
        Ss = [s_ref[h] for h in hs]
        o_blocks = [[] for _ in hs]
        for r0 in range(0, tm, L):
            rows = slice(r0, r0 + L)
            vhs = [vb[r0:r0 + vb.shape[0] - tm + L, i * RET_DV:(i + 1) * RET_DV] for i in range(nh)]
            atts = [(_dot_nt(qb[rows, kc], kb[rows, kc]) * dintra_ref[h]).astype(BF16) for h, kc in zip(hs, kcs)]
            os_ = [_dot(att, vh[:L]) + _dot(qb[rows, kc], S.astype(BF16))
                   * jnp.concatenate([dquery_ref[h]] * (RET_DV // LANES), axis=1)
                   for att, vh, kc, S, h in zip(atts, vhs, kcs, Ss, hs)]
            k_decs = [_pad_rows(k[rows, kc] * jnp.concatenate([dkey_ref[h]] * (RET_DK // LANES), axis=1), LANES)
                      for h, kc in zip(hs, kcs)]
            Ss = [S * math.exp(log_decay[h] * L) + _dot(k_dec.T.astype(BF16), vh)
                  for h, S, k_dec, vh in zip(hs, Ss, k_decs, vhs)]
            ocs = [o - jnp.mean(o, axis=-1, keepdims=True) for o in os_]
            for i, oc in enumerate(ocs):
                o_blocks[i].append(oc * lax.rsqrt(jnp.mean(oc * oc, axis=-1, keepdims=True) + GN_EPS))
        for i, h in enumerate(hs):
            s_ref[h] = Ss[i]
            o = o_blocks[i][0] if len(o_blocks[i]) == 1 else jnp.concatenate(o_blocks[i], axis=0)
            o_heads.append((o * gate[:, i * RET_DV:(i + 1) * RET_DV]).astype(BF16))
    return o_heads


def _ret_heads_separate(qb, kb, k, vb, s0_ref, s_ref, dintra_ref, dquery_ref, dkey_ref, log_decay, L):
    tm = qb.shape[0]
    o_heads = []
    for h0 in range(0, RET_HEADS, RET_LOCKSTEP):
        hs = list(range(h0, min(h0 + RET_LOCKSTEP, RET_HEADS)))
        kcs = [slice(h * RET_DK, (h + 1) * RET_DK) for h in hs]
        vcs = [slice(h * RET_DV, (h + 1) * RET_DV) for h in hs]
        dqs = [jnp.concatenate([dquery_ref[h]] * (RET_DV // LANES), axis=1) for h in hs]
        dks = [jnp.concatenate([dkey_ref[h]] * (RET_DK // LANES), axis=1) for h in hs]
        atts = [(_dot_nt(qb[:, kc], kb[:, kc]) * dintra_ref[h]).astype(BF16) for h, kc in zip(hs, kcs)]
        k_decs = [k[:, kc] * dk for kc, dk in zip(kcs, dks)]
        o_inters = [[] for _ in hs]
        for c in range(tm // L):
            crow = slice(c * L, (c + 1) * L)
            Ss = [s0_ref[c, h] for h in hs]
            for i in range(len(hs)):
                o_inters[i].append(_dot(qb[crow, kcs[i]], Ss[i].astype(BF16)) * dqs[i][crow])
            for h, S, k_dec, vc in zip(hs, Ss, k_decs, vcs):
                s_ref[c, h] = (S * math.exp(log_decay[h] * L)
                               + _dot(_pad_rows(k_dec[crow], LANES).T.astype(BF16), _pad_rows(vb[crow, vc], LANES)))
        os_ = [_dot(att, vb[:, vc]) + (oi[0] if len(oi) == 1 else jnp.concatenate(oi, axis=0))
               for att, vc, oi in zip(atts, vcs, o_inters)]
        ocs = [o - jnp.mean(o, axis=-1, keepdims=True) for o in os_]
        o_heads += [oc * lax.rsqrt(jnp.mean(oc * oc, axis=-1, keepdims=True) + GN_EPS) for oc in ocs]
    return o_heads


def _ret_kernel(*refs, L, chained):
    if chained:
        (x_ref, mod_ref, ng_ref, wq_ref, wk_ref, wv_ref, wg_ref, wo_ref, y_ref, s_ref,
         dintra_ref, dquery_ref, dkey_ref) = refs
    else:
        (x_ref, mod_ref, ng_ref, wq_ref, wk_ref, wv_ref, wg_ref, wo_ref, s0_ref, y_ref, s_ref,
         dintra_ref, dquery_ref, dkey_ref) = refs
    rb = dintra_ref.shape[1]
    log_decay = [math.log(1.0 - 2.0 ** (-5.0 - h)) for h in range(RET_HEADS)]

    @pl.when((pl.program_id(0) == 0) & (pl.program_id(1) == 0))
    def _():
        ri = lax.broadcasted_iota(jnp.int32, (rb, rb), 0)
        ci = lax.broadcasted_iota(jnp.int32, (rb, rb), 1)
        rel = (ri - ci).astype(F32)
        keep = (ri >= ci) & (ci >= (ri // L) * L)
        pos = (lax.broadcasted_iota(jnp.int32, (rb, LANES), 0) % L).astype(F32)
        for h in range(RET_HEADS):
            dintra_ref[h] = jnp.where(keep, jnp.exp(log_decay[h] * jnp.maximum(rel, 0.0)), 0.0)
            dquery_ref[h] = jnp.exp(log_decay[h] * (pos + 1.0))
            dkey_ref[h] = jnp.exp(log_decay[h] * (L - 1.0 - pos))

    if chained:
        @pl.when(pl.program_id(1) == 0)
        def _():
            s_ref[...] = jnp.zeros(s_ref.shape, F32)

    x = x_ref[...]
    hb = _modulated_norm(x, ng_ref[...], mod_ref[0], mod_ref[1]).astype(BF16)
    if chained:
        o_heads = _ret_heads_chained(hb, wq_ref, wk_ref, wv_ref, wg_ref, s_ref,
                                     dintra_ref, dquery_ref, dkey_ref, log_decay, L)
        o = jnp.concatenate(o_heads, axis=1)
    else:
        k = _dot(hb, wk_ref[...]) * (RET_DK ** -0.5)
        o_heads = _ret_heads_separate(_dot(hb, wq_ref[...]).astype(BF16), k.astype(BF16), k,
                                      _dot(hb, wv_ref[...]).astype(BF16), s0_ref, s_ref,
                                      dintra_ref, dquery_ref, dkey_ref, log_decay, L)
        o = (jnp.concatenate(o_heads, axis=1) * _silu(_dot(hb, wg_ref[...]))).astype(BF16)
    y_ref[...] = x + mod_ref[2] * _dot(o, wo_ref[...])


def _ret_layer(x, mod, ng, w, s0, tm, L):
    B, T, D = x.shape
    chained = s0 is None
    rb = L if chained else tm
    weights = [ng] + list(w)
    in_specs = [
        pl.BlockSpec((None, tm, D), lambda b, t: (b, t, 0)),
        pl.BlockSpec((None, ADA_CHUNKS, mod.shape[2], D), lambda b, t: (b, 0, 0, 0)),
    ] + [_resident(a.shape) for a in weights]
    args = [x, mod] + weights
    if chained:
        assert tm % L == 0
        state_shape = (B, RET_HEADS, RET_DK, RET_DV)
        state_spec = pl.BlockSpec((None,) + state_shape[1:], lambda b, t: (b, 0, 0, 0))
    else:
        assert tm == T and T % L == 0 and s0.shape[0] * L == B * T
        state_shape = s0.shape
        state_spec = pl.BlockSpec((T // L,) + state_shape[1:], lambda b, t: (b, 0, 0, 0))
        in_specs.append(state_spec)
        args.append(s0)
    return pl.pallas_call(
        functools.partial(_ret_kernel, L=L, chained=chained),
        grid=(B, T // tm),
        in_specs=in_specs,
        out_specs=[pl.BlockSpec((None, tm, D), lambda b, t: (b, t, 0)), state_spec],
        out_shape=[jax.ShapeDtypeStruct((B, T, D), F32), jax.ShapeDtypeStruct(state_shape, F32)],
        scratch_shapes=[pltpu.VMEM((RET_HEADS, rb, rb), F32), pltpu.VMEM((RET_HEADS, rb, LANES), F32),
                        pltpu.VMEM((RET_HEADS, rb, LANES), F32)],
        compiler_params=_params(("arbitrary", "arbitrary")),
        name="ret_layer",
    )(*args)


def _head_mean_square(a, hsum):
    w = hsum.shape[0]
    parts = [_dot((a[:, j:j + w] * a[:, j:j + w]).astype(BF16), hsum) for j in range(0, a.shape[1], w)]
    return parts[0] if len(parts) == 1 else jnp.concatenate(parts, axis=1)


def _tile_head_lanes(a, kv):
    t = a[:, (kv // 2) * LANES:(kv // 2 + 1) * LANES]
    low = lax.broadcasted_iota(jnp.int32, t.shape, 1) < SWA_HD
    keep = low if kv % 2 == 0 else jnp.logical_not(low)
    return jnp.where(keep, t, pltpu.roll(t, SWA_HD, axis=1)).astype(BF16)


def _swa_kernel(*refs, Lq, is_prompt):
    if is_prompt:
        (sink_ref, x_ref, mod_ref, ng_ref, wqkv_ref, qn_ref, kn_ref, wo_ref, y_ref, kc_ref, vc_ref,
         bias_ref) = refs
    else:
        (sink_ref, x_ref, mod_ref, ng_ref, wqkv_ref, qn_ref, kn_ref, wo_ref, ck_ref, cv_ref,
         y_ref, kc_ref, vc_ref, bias_ref) = refs
    t = pl.program_id(1)
    x = x_ref[...]
    tm = x.shape[0]
    nq = SWA_Q_HEADS * SWA_HD
    nk = SWA_KV_HEADS * SWA_HD
    gw = SWA_GROUP * SWA_HD
    nkeys = WINDOW + Lq
    nkp = nkeys + 1 + (-(nkeys + 1)) % LANES
    rows_q = SWA_GROUP * Lq

    @pl.when((pl.program_id(0) == 0) & (t == 0))
    def _():
        ri = lax.broadcasted_iota(jnp.int32, (rows_q, nkp), 0)
        ji = lax.broadcasted_iota(jnp.int32, (rows_q, nkp), 1)
        grp = ri // Lq
        dist = jnp.abs((ri - grp * Lq) - ji + WINDOW).astype(F32)
        for kv in range(SWA_KV_HEADS):
            sink = jnp.zeros((rows_q, nkp), F32)
            for gi in range(SWA_GROUP):
                sink = jnp.where(grp == gi, sink_ref[kv * SWA_GROUP + gi] * LOG2E, sink)
            bias = -LOG2E * jnp.exp2(-0.5 * (kv * SWA_GROUP + grp + 1).astype(F32)) * dist
            bias_ref[kv] = jnp.where(ji < nkeys, bias, jnp.where(ji == nkeys, sink, NEG_INF))

    hb = _modulated_norm(x, ng_ref[...], mod_ref[0], mod_ref[1]).astype(BF16)
    hi = lax.broadcasted_iota(jnp.int32, (gw, gw), 0) // SWA_HD
    hj = lax.broadcasted_iota(jnp.int32, (gw, gw), 1) // SWA_HD
    hsum = jnp.where(hi == hj, 1.0 / SWA_HD, 0.0).astype(BF16)
    kv_new = _dot(hb, wqkv_ref[:, nq:])
    k = kv_new[:, :nk]
    v_new = kv_new[:, nk:]
    k_new = k * lax.rsqrt(_head_mean_square(k, hsum) + NORM_EPS) * kn_ref[...]
    q_gain = qn_ref[...] * (SWA_HD ** -0.5 * LOG2E)
    qs_kv = []
    for kv0 in range(0, SWA_KV_HEADS, 2):
        qg = _dot(hb, wqkv_ref[:, kv0 * gw:(kv0 + 2) * gw])
        qg = (qg * lax.rsqrt(_head_mean_square(qg, hsum) + NORM_EPS) * q_gain[:, kv0 * gw:(kv0 + 2) * gw]).astype(BF16)
        qs_kv += [qg[:, :gw], qg[:, gw:]]

    nchunk = tm // Lq
    k_pad = jnp.zeros((nkp - nkeys, nk), F32)
    if is_prompt:
        @pl.when(t == 0)
        def _():
            kc_ref[...] = jnp.zeros(kc_ref.shape, F32)
            vc_ref[...] = jnp.zeros(vc_ref.shape, F32)
        k_exts = [jnp.concatenate([kc_ref[...], k_new, k_pad], axis=0)]
        v_exts = [jnp.concatenate([vc_ref[...], v_new], axis=0)]
        src = [(0, c * Lq) for c in range(nchunk)]
    else:
        k_exts = [jnp.concatenate([ck_ref[c], k_new[c * Lq:(c + 1) * Lq], k_pad], axis=0) for c in range(nchunk)]
        v_exts = [jnp.concatenate([cv_ref[c], v_new[c * Lq:(c + 1) * Lq]], axis=0) for c in range(nchunk)]
        src = [(c, 0) for c in range(nchunk)]

    low_half = lax.broadcasted_iota(jnp.int32, (Lq, LANES), 1) < SWA_HD
    ji = lax.broadcasted_iota(jnp.int32, (rows_q, nkp), 1)
    is_key = ji < nkeys
    zero = jnp.zeros((), BF16)

    o_cols = []
    for kv in range(SWA_KV_HEADS):
        kts = [_tile_head_lanes(k_ext, kv) for k_ext in k_exts]
        vts = [_tile_head_lanes(v_ext, kv) for v_ext in v_exts]
        bias = bias_ref[kv]
        o_rows = []
        for c0 in range(0, nchunk, SWA_LOCKSTEP):
            cs = list(range(c0, min(c0 + SWA_LOCKSTEP, nchunk)))
            r0s = [c * Lq for c in cs]
            kbs = [kts[src[c][0]][src[c][1]:src[c][1] + nkp] for c in cs]
            vbs = [vts[src[c][0]][src[c][1]:src[c][1] + nkeys] for c in cs]
            qss = []
            for r0 in r0s:
                qs = []
                for gi in range(SWA_GROUP):
                    qt = qs_kv[kv][r0:r0 + Lq, (gi // 2) * LANES:(gi // 2 + 1) * LANES]
                    qs.append(jnp.where(low_half if gi % 2 == 0 else jnp.logical_not(low_half), qt, zero))
                qss.append(jnp.concatenate(qs, axis=0))
            logits = [jnp.where(is_key, _dot_nt(qs, kb), 0.0) + bias for qs, kb in zip(qss, kbs)]
            if is_prompt:
                logits = [jnp.where(ji >= WINDOW - r0 - t * tm, lg, NEG_INF) if r0 < WINDOW else lg
                          for lg, r0 in zip(logits, r0s)]
            ps = [jnp.exp2(lg - jnp.max(lg, axis=-1, keepdims=True)) for lg in logits]
            dens = [jnp.sum(p, axis=-1, keepdims=True) for p in ps]
            ocs = [_dot(p[:, :nkeys].astype(BF16), vb) * (1.0 / den) for p, vb, den in zip(ps, vbs, dens)]
            for oc in ocs:
                o_rows.append(jnp.concatenate(
                    [jnp.where(low_half, oc[2 * j * Lq:(2 * j + 1) * Lq], oc[(2 * j + 1) * Lq:(2 * j + 2) * Lq])
                     for j in range(SWA_GROUP // 2)], axis=1))
        o_cols.append(o_rows[0] if len(o_rows) == 1 else jnp.concatenate(o_rows, axis=0))
    o = jnp.concatenate(o_cols, axis=1)
    y_ref[...] = x + mod_ref[2] * _dot(o.astype(BF16), wo_ref[...])
    if is_prompt:
        kc_ref[...] = k_new[tm - WINDOW:]
        vc_ref[...] = v_new[tm - WINDOW:]
    else:
        kc_ref[...] = k_new
        vc_ref[...] = v_new


def _swa_layer(x, mod, ng, w, cache, tm, Lq):
    B, T, D = x.shape
    is_prompt = cache is None
    wqkv, qn, kn, sinks, wo = w
    nk = SWA_KV_HEADS * SWA_HD
    in_specs = [
        pl.BlockSpec(memory_space=pltpu.SMEM),
        pl.BlockSpec((None, tm, D), lambda b, t: (b, t, 0)),
        pl.BlockSpec((None, ADA_CHUNKS, mod.shape[2], D), lambda b, t: (b, 0, 0, 0)),
        _resident((1, D)),
        _resident(wqkv.shape), _resident(qn.shape), _resident(kn.shape), _resident(wo.shape),
    ]
    args = [sinks, x, mod, ng, wqkv, qn, kn, wo]
    if is_prompt:
        rows_out = WINDOW
    else:
        ck, cv = cache
        assert B == 1 and tm == T and ck.shape[0] * Lq == T
        in_specs += [pl.BlockSpec(ck.shape, lambda b, t: (0, 0, 0))] * 2
        args += [ck, cv]
        rows_out = T
    cache_spec = pl.BlockSpec((None, rows_out, nk), lambda b, t: (b, 0, 0))
    nkp = WINDOW + Lq + 1 + (-(WINDOW + Lq + 1)) % LANES
    return pl.pallas_call(
        functools.partial(_swa_kernel, Lq=Lq, is_prompt=is_prompt),
        grid=(B, T // tm),
        in_specs=in_specs,
        out_specs=[pl.BlockSpec((None, tm, D), lambda b, t: (b, t, 0)), cache_spec, cache_spec],
        out_shape=[jax.ShapeDtypeStruct((B, T, D), F32),
                   jax.ShapeDtypeStruct((B, rows_out, nk), F32),
                   jax.ShapeDtypeStruct((B, rows_out, nk), F32)],
        scratch_shapes=[pltpu.VMEM((SWA_KV_HEADS, SWA_GROUP * Lq, nkp), F32)],
        compiler_params=_params(("arbitrary", "arbitrary")),
        name="swa_layer",
    )(*args)


def _tile(T, pref):
    return pref if T % pref == 0 else T


def kernel(x_prompt, x_sample, state_gla_l0, state_ret_l1, cache_swa_k_l2, cache_swa_v_l2, state_gla_l3, c_prompt, c_sample, norm_mix, norm_mlp, ada_w, ada_b, mlp_w1, mlp_w2, gla_wq, gla_wk, gla_wv, gla_wg1, gla_wg2, gla_bg, gla_wr, gla_norm, gla_wo, ret_wq, ret_wk, ret_wv, ret_wg, ret_wo, swa_wqkv, swa_qnorm, swa_knorm, swa_sinks, swa_wo):
    bp = x_prompt.shape[0]
    bs = x_sample.shape[0]
    Tp = x_prompt.shape[1]
    Ts = x_sample.shape[1]
    gla_states = [state_gla_l0, state_gla_l3]
    ret_states = [state_ret_l1]
    swa_caches = [(cache_swa_k_l2, cache_swa_v_l2)]

    mod_all = _ada_modulation(jnp.concatenate([c_prompt, c_sample], axis=0), ada_w, ada_b)
    mod_all = mod_all.reshape(DEPTH, bp + bs, ADA_CHUNKS, D_MODEL)

    w1_all = mlp_w1.astype(BF16)
    w2_all = mlp_w2.astype(BF16)
    yp, ys = x_prompt, x_sample
    new_state = []
    for layer in range(DEPTH):
        kind, j = layer % N_MIXERS, layer // N_MIXERS
        mod_p = mod_all[layer, :bp]
        mod_s = mod_all[layer, bp:]
        mod_rows = jnp.repeat(mod_s.transpose(1, 0, 2), Ts, axis=1)[None]
        ng = norm_mix[layer].reshape(1, D_MODEL)
        if kind == 0:
            wqk = jnp.concatenate([gla_wq[j].reshape(D_MODEL, GLA_HEADS, GLA_DK),
                                   gla_wk[j].reshape(D_MODEL, GLA_HEADS, GLA_DK)], axis=2).reshape(D_MODEL, -1)
            w = (wqk.astype(BF16), gla_wv[j].astype(BF16), gla_wg1[j].astype(BF16),
                 gla_wg2[j].astype(BF16), gla_bg[j].reshape(1, -1), gla_wr[j].astype(BF16),
                 gla_norm[j].reshape(1, -1), gla_wo[j].astype(BF16))
            yp, st_p = _gla_layer(yp, mod_p[:, :, None, :], ng, w, None, _tile(Tp, 1024), min(CHUNK, Tp))
            assert Ts <= CHUNK
            ys, st_s = _gla_layer(ys.reshape(1, bs * Ts, D_MODEL), mod_rows, ng, w, gla_states[j], bs * Ts, Ts)
            ys = ys.reshape(bs, Ts, D_MODEL)
            new_state += [st_p, st_s]
        elif kind == 1:
            w = (ret_wq[j].astype(BF16), ret_wk[j].astype(BF16), ret_wv[j].astype(BF16), ret_wg[j].astype(BF16),
                 ret_wo[j].astype(BF16))
            yp, st_p = _ret_layer(yp, mod_p[:, :, None, :], ng, w, None, _tile(Tp, 1024), min(RET_RB, Tp))
            assert Ts <= CHUNK
            gs = RET_SAMPLE_GROUP if bs % RET_SAMPLE_GROUP == 0 else bs
            mod_g = jnp.repeat(mod_s.reshape(bs // gs, gs, ADA_CHUNKS, D_MODEL).transpose(0, 2, 1, 3), Ts, axis=2)
            ys, st_s = _ret_layer(ys.reshape(bs // gs, gs * Ts, D_MODEL), mod_g, ng, w, ret_states[j], gs * Ts, Ts)
            ys = ys.reshape(bs, Ts, D_MODEL)
            new_state += [st_p, st_s]
        else:
            w = (swa_wqkv[j].astype(BF16), jnp.tile(swa_qnorm[j], SWA_Q_HEADS).reshape(1, -1),
                 jnp.tile(swa_knorm[j], SWA_KV_HEADS).reshape(1, -1), swa_sinks[j], swa_wo[j].astype(BF16))
            yp, k_p, v_p = _swa_layer(yp, mod_p[:, :, None, :], ng, w, None, _tile(Tp, 1024), CHUNK)
            assert Ts <= CHUNK
            nk = SWA_KV_HEADS * SWA_HD
            ck, cv = (c.reshape(bs, WINDOW, nk) for c in swa_caches[j])
            ys, k_s, v_s = _swa_layer(ys.reshape(1, bs * Ts, D_MODEL), mod_rows, ng, w, (ck, cv), bs * Ts, Ts)
            ys = ys.reshape(bs, Ts, D_MODEL)
            new_state += [a.reshape(bp, WINDOW, SWA_KV_HEADS, SWA_HD) for a in (k_p, v_p)]
            new_state += [a.reshape(bs, Ts, SWA_KV_HEADS, SWA_HD) for a in (k_s, v_s)]
        gm = norm_mlp[layer].reshape(1, D_MODEL)
        yp = _mlp_layer(yp, mod_p[:, :, None, :], gm, w1_all, w2_all, layer, _tile(Tp, 1024))
        ys = _mlp_layer(ys.reshape(1, bs * Ts, D_MODEL), mod_rows, gm, w1_all, w2_all, layer,
                        bs * Ts).reshape(bs, Ts, D_MODEL)
    return (yp, ys, *new_state)
```
